```python
import jax, jax.numpy as jnp
from jax import lax
import numpy as np

D_MODEL = 1024
BATCH = 8
SEQ = 8192
DEPTH = 1

D_POOL = D_MODEL // 4
POOL_WINDOWS = (2, 4, 8, 16)
N_POOL_GROUPS = len(POOL_WINDOWS)
POOL_GROUP = D_POOL // N_POOL_GROUPS
V_HEAD = 128
QK_NOPE = 128
QK_ROPE = 64
N_HEADS = (D_MODEL - D_POOL) // V_HEAD
D_ATT = N_HEADS * V_HEAD
Q_LORA = 384
KV_LORA = 256
D_IN = D_POOL + Q_LORA + KV_LORA + QK_ROPE
ROPE_THETA = 10000.0
Q_BLOCK = 128
PEER_HEADS = 8
PEER_NKEYS = 128
PEER_EXPERTS = PEER_NKEYS * PEER_NKEYS
PEER_DQ = 256
PEER_HALF = PEER_DQ // 2
PEER_TOPK = 16
TOK_CHUNK = 128
ALPHA = (2.0 * DEPTH) ** 0.25
BETA = (8.0 * DEPTH) ** -0.25
LN_EPS = 1e-5
RMS_EPS = 1e-6

kernel_name = 'hybrid_pool_mla_peer_deepnorm_adaln'


def layer_norm(x, g, b):
    xf = x.astype(jnp.float32)
    mu = jnp.mean(xf, axis=-1, keepdims=True)
    var = jnp.mean(jnp.square(xf - mu), axis=-1, keepdims=True)
    return ((xf - mu) * lax.rsqrt(var + LN_EPS) * g.astype(jnp.float32) + b.astype(jnp.float32)).astype(x.dtype)


def rms_norm(x, g):
    xf = x.astype(jnp.float32)
    y = xf * lax.rsqrt(jnp.mean(jnp.square(xf), axis=-1, keepdims=True) + RMS_EPS)
    return (y * g.astype(jnp.float32)).astype(x.dtype)


def rope(x, cos, sin):
    h = x.shape[-1] // 2
    x1, x2 = x[..., :h], x[..., h:]
    return jnp.concatenate([x1 * cos - x2 * sin, x2 * cos + x1 * sin], axis=-1)


def causal_multiscale_pool(p, w_pool, pool_scale):
    B, S, _ = p.shape
    g = p.astype(jnp.float32).reshape(B, S, N_POOL_GROUPS, POOL_GROUP)
    cs = jnp.concatenate([jnp.zeros_like(g[:, :1]), jnp.cumsum(g, axis=1)], axis=1)
    t = jnp.arange(S)[:, None]
    win = jnp.array(POOL_WINDOWS, dtype=jnp.int32)[None, :]
    start = jnp.maximum(t + 1 - win, 0)
    gid = jnp.arange(N_POOL_GROUPS)[None, :]
    win_sum = cs[:, 1:] - cs[:, start, gid]
    count = jnp.minimum(t + 1, win).astype(jnp.float32)
    mixed = win_sum / count[None, :, :, None] - g
    y = jnp.einsum('bsgc,gcd->bsgd', mixed.astype(p.dtype), w_pool)
    y = y * pool_scale.reshape(N_POOL_GROUPS, POOL_GROUP)
    return y.reshape(B, S, D_POOL)


def mla_attention(cq, ckv, kr, cos, sin, q_norm_g, w_uq, kv_norm_g, w_ukv):
    B, S, _ = cq.shape
    q = (rms_norm(cq, q_norm_g) @ w_uq).reshape(B, S, N_HEADS, QK_NOPE + QK_ROPE)
    q_nope = q[..., :QK_NOPE]
    q_rope = rope(q[..., QK_NOPE:], cos[:, :, None], sin[:, :, None])
    kv = (rms_norm(ckv, kv_norm_g) @ w_ukv).reshape(B, S, N_HEADS, QK_NOPE + V_HEAD)
    k_nope, v = kv[..., :QK_NOPE], kv[..., QK_NOPE:]
    k_rope = rope(kr, cos, sin)
    scale = (QK_NOPE + QK_ROPE) ** -0.5
    nb = S // Q_BLOCK
    k_idx = jnp.arange(S)

    def attend(args):
        qn, qr, i = args
        s = (jnp.einsum('bqhd,bkhd->bhqk', qn, k_nope).astype(jnp.float32)
             + jnp.einsum('bqhr,bkr->bhqk', qr, k_rope).astype(jnp.float32)) * scale
        q_idx = i * Q_BLOCK + jnp.arange(Q_BLOCK)
        s = jnp.where(q_idx[:, None] >= k_idx[None, :], s, -jnp.inf)
        pr = jax.nn.softmax(s, axis=-1)
        return jnp.einsum('bhqk,bkhd->bqhd', pr.astype(v.dtype), v)

    qn_b = q_nope.reshape(B, nb, Q_BLOCK, N_HEADS, QK_NOPE).transpose(1, 0, 2, 3, 4)
    qr_b = q_rope.reshape(B, nb, Q_BLOCK, N_HEADS, QK_ROPE).transpose(1, 0, 2, 3, 4)
    out = lax.map(attend, (qn_b, qr_b, jnp.arange(nb)))
    return out.transpose(1, 0, 2, 3, 4).reshape(B, S, D_ATT)


def peer(h, w_peer_q, peer_keys, peer_u, peer_v):
    B, S, D = h.shape
    tokens = h.reshape((B * S) // TOK_CHUNK, TOK_CHUNK, D)

    def chunk(xc):
        q = (xc @ w_peer_q).reshape(TOK_CHUNK, PEER_HEADS, 2, PEER_HALF)
        sc = jnp.einsum('thpd,hpnd->thpn', q, peer_keys).astype(jnp.float32)
        s1, i1 = lax.top_k(sc[:, :, 0], PEER_TOPK)
        s2, i2 = lax.top_k(sc[:, :, 1], PEER_TOPK)
        cand = (s1[..., :, None] + s2[..., None, :]).reshape(TOK_CHUNK, PEER_HEADS, PEER_TOPK * PEER_TOPK)
        cidx = (i1[..., :, None] * PEER_NKEYS + i2[..., None, :]).reshape(TOK_CHUNK, PEER_HEADS, PEER_TOPK * PEER_TOPK)
        s, j = lax.top_k(cand, PEER_TOPK)
        idx = jnp.take_along_axis(cidx, j, axis=-1)
        gate = jax.nn.softmax(s, axis=-1)
        u = peer_u[idx]
        a = jax.nn.gelu(jnp.einsum('thkd,td->thk', u, xc).astype(jnp.float32), approximate=False)
        w = (gate * a).astype(xc.dtype)
        return jnp.einsum('thk,thkd->td', w, peer_v[idx])

    out = lax.map(chunk, tokens)
    return out.reshape(B, S, D)


def setup_inputs(seed: int = 0) -> dict:
    key = jax.random.key(seed)
    ks = jax.random.split(key, 24)
    L = DEPTH

    def nrm(k, shape, scale):
        return jax.random.normal(k, shape, jnp.float32) * scale

    x = nrm(ks[0], (BATCH, SEQ, D_MODEL), 1.0)
    c = nrm(ks[1], (BATCH, D_MODEL), 1.0)
    positions = jnp.broadcast_to(jnp.arange(SEQ, dtype=jnp.int32)[None, :], (BATCH, SEQ))
    w_ada = nrm(ks[2], (L, D_MODEL, 6 * D_MODEL), D_MODEL ** -0.5)
    b_ada = nrm(ks[3], (L, 6 * D_MODEL), 0.02)
    w_in = nrm(ks[4], (L, D_MODEL, D_IN), D_MODEL ** -0.5)
    pool_w = nrm(ks[5], (L, N_POOL_GROUPS, POOL_GROUP, POOL_GROUP), BETA * POOL_GROUP ** -0.5)
    pool_scale = 1.0 + nrm(ks[6], (L, D_POOL), 0.02)
    q_norm_g = 1.0 + nrm(ks[7], (L, Q_LORA), 0.02)
    w_uq = nrm(ks[8], (L, Q_LORA, N_HEADS * (QK_NOPE + QK_ROPE)), Q_LORA ** -0.5)
    kv_norm_g = 1.0 + nrm(ks[9], (L, KV_LORA), 0.02)
    w_uk = nrm(ks[10], (L, KV_LORA, N_HEADS, QK_NOPE), KV_LORA ** -0.5)
    w_uv = nrm(ks[11], (L, KV_LORA, N_HEADS, V_HEAD), BETA * KV_LORA ** -0.5)
    w_ukv = jnp.concatenate([w_uk, w_uv], axis=-1).reshape(L, KV_LORA, N_HEADS * (QK_NOPE + V_HEAD))
    w_out = nrm(ks[12], (L, D_POOL + D_ATT, D_MODEL), BETA * (D_POOL + D_ATT) ** -0.5)
    ln1_g = 1.0 + nrm(ks[13], (L, D_MODEL), 0.02)
    ln1_b = nrm(ks[14], (L, D_MODEL), 0.02)
    w_peer_q = nrm(ks[15], (L, D_MODEL, PEER_HEADS * PEER_DQ), D_MODEL ** -0.5)
    peer_keys = nrm(ks[16], (L, PEER_HEADS, 2, PEER_NKEYS, PEER_HALF), PEER_HALF ** -0.5)
    peer_u = nrm(ks[17], (L, PEER_EXPERTS, D_MODEL), D_MODEL ** -0.5)
    peer_v = nrm(ks[18], (L, PEER_EXPERTS, D_MODEL), BETA * (PEER_HEADS * PEER_TOPK) ** -0.5)
    ln2_g = 1.0 + nrm(ks[19], (L, D_MODEL), 0.02)
    ln2_b = nrm(ks[20], (L, D_MODEL), 0.02)
    return {'x': x, 'c': c, 'positions': positions, 'w_ada': w_ada, 'b_ada': b_ada,
            'w_in': w_in, 'pool_w': pool_w, 'pool_scale': pool_scale, 'q_norm_g': q_norm_g,
            'w_uq': w_uq, 'kv_norm_g': kv_norm_g, 'w_ukv': w_ukv, 'w_out': w_out,
            'ln1_g': ln1_g, 'ln1_b': ln1_b, 'w_peer_q': w_peer_q, 'peer_keys': peer_keys,
            'peer_u': peer_u, 'peer_v': peer_v, 'ln2_g': ln2_g, 'ln2_b': ln2_b}


def reference(x, c, positions, w_ada, b_ada, w_in, pool_w, pool_scale, q_norm_g, w_uq,
              kv_norm_g, w_ukv, w_out, ln1_g, ln1_b, w_peer_q, peer_keys, peer_u, peer_v,
              ln2_g, ln2_b):
    inv_freq = ROPE_THETA ** (-jnp.arange(0, QK_ROPE, 2, dtype=jnp.float32) / QK_ROPE)
    ang = positions.astype(jnp.float32)[..., None] * inv_freq
    cos = jnp.cos(ang).astype(x.dtype)
    sin = jnp.sin(ang).astype(x.dtype)
    c_act = jax.nn.silu(c)
    o1 = D_POOL
    o2 = o1 + Q_LORA
    o3 = o2 + KV_LORA
    for l in range(DEPTH):
        mod = c_act @ w_ada[l] + b_ada[l]
        sh1, sc1, g1, sh2, sc2, g2 = [m[:, None, :] for m in jnp.split(mod, 6, axis=-1)]
        h = x * (1.0 + sc1) + sh1
        z = h @ w_in[l]
        pool_out = causal_multiscale_pool(z[..., :o1], pool_w[l], pool_scale[l])
        att_out = mla_attention(z[..., o1:o2], z[..., o2:o3], z[..., o3:], cos, sin,
                                q_norm_g[l], w_uq[l], kv_norm_g[l], w_ukv[l])
        mix = jnp.concatenate([pool_out, att_out], axis=-1) @ w_out[l]
        x = layer_norm(ALPHA * x + g1 * mix, ln1_g[l], ln1_b[l])
        h2 = x * (1.0 + sc2) + sh2
        ffn = peer(h2, w_peer_q[l], peer_keys[l], peer_u[l], peer_v[l])
        x = layer_norm(ALPHA * x + g2 * ffn, ln2_g[l], ln2_b[l])
    return x
```

```python
import functools
import math

import jax
import jax.numpy as jnp
from jax import lax
from jax.experimental import pallas as pl
from jax.experimental.pallas import tpu as pltpu

POOL_WINDOWS = (2, 4, 8, 16)
POOL_GROUP = 64
D_POOL = POOL_GROUP * len(POOL_WINDOWS)
V_HEAD = 128
QK_NOPE = 128
QK_ROPE = 64
QK_DIM = QK_NOPE + QK_ROPE
N_HEADS = 6
D_ATT = N_HEADS * V_HEAD
Q_LORA = 384
KV_LORA = 256
ROPE_THETA = 10000.0
PEER_HEADS = 8
PEER_NKEYS = 128
PEER_HALF = 128
PEER_TOPK = 16
DEPTH = 1
ALPHA = (2.0 * DEPTH) ** 0.25
LN_EPS = 1e-5
RMS_EPS = 1e-6

LANES = 128
VMEM_LIMIT_BYTES = 56 * 1024 * 1024

MXU_DTYPE = jnp.bfloat16
F32 = jnp.float32
NOT_SELECTED = 1.0e4

POOL_HALO = 16
TOKEN_TILE = 512
ATTN_BLOCK = 512
PEER_ROWS_PER_CHUNK = 8


def _dot(a, b):
    return jnp.dot(a, b, preferred_element_type=F32)


def _dot_nt(a, b):
    return lax.dot_general(a, b, (((1,), (1,)), ((), ())), preferred_element_type=F32)


def _layer_norm(y, g, b):
    mu = jnp.mean(y, axis=-1, keepdims=True)
    d = y - mu
    var = jnp.mean(d * d, axis=-1, keepdims=True)
    return d * lax.rsqrt(var + LN_EPS) * g + b


def _rms_norm(x, g):
    ms = jnp.mean(x * x, axis=-1, keepdims=True)
    return x * lax.rsqrt(ms + RMS_EPS) * g


def _adaln_kernel(c_ref, w_ref, b_ref, o_ref):
    c = c_ref[...]
    c_act = c * jax.nn.sigmoid(c)
    o_ref[...] = _dot(c_act.astype(MXU_DTYPE), w_ref[...].astype(MXU_DTYPE)) + b_ref[...]


def _adaln(c, w_ada, b_ada):
    bsz, d = c.shape
    n = w_ada.shape[1]
    blk = d
    return pl.pallas_call(
        _adaln_kernel,
        grid=(n // blk,),
        in_specs=[pl.BlockSpec((bsz, d), lambda i: (0, 0)),
                  pl.BlockSpec((d, blk), lambda i: (0, i)),
                  pl.BlockSpec((1, blk), lambda i: (0, i))],
        out_specs=pl.BlockSpec((bsz, blk), lambda i: (0, i)),
        out_shape=jax.ShapeDtypeStruct((bsz, n), F32),
        compiler_params=pltpu.CompilerParams(dimension_semantics=("arbitrary",),
                                             vmem_limit_bytes=VMEM_LIMIT_BYTES),
        name="adaln_mod",
    )(c, w_ada, b_ada.reshape(1, n))


def _inproj_kernel(x_ref, mod_ref, pos_ref, invf_ref, win_ref, wpool_ref, pscale_ref, qg_ref,
                   wuq_ref, kvg_ref, wukv_ref, pool_ref, q_ref, k_ref, v_ref, ext_ref):
    j = pl.program_id(1)
    tt = x_ref.shape[1]
    x = x_ref[0]
    sh1 = mod_ref[0, 0:1, :]
    sc1 = mod_ref[0, 1:2, :]
    h = x * (1.0 + sc1) + sh1
    z = _dot(h.astype(MXU_DTYPE), win_ref[...])

    @pl.when(j == 0)
    def _():
        ext_ref[0:POOL_HALO, :] = jnp.zeros((POOL_HALO, D_POOL), F32)

    p = z[:, :D_POOL]
    ext_ref[POOL_HALO:POOL_HALO + tt, :] = p

    def shifted(d, col):
        return ext_ref[pl.ds(POOL_HALO - d, tt), col * LANES:(col + 1) * LANES]

    lane = lax.broadcasted_iota(jnp.int32, (tt, LANES), 1)
    low = lane < POOL_GROUP
    t_seq = (j * tt + lax.broadcasted_iota(jnp.int32, (tt, LANES), 0) + 1).astype(F32)
    mixed = []
    for col in range(2):
        w_lo, w_hi = POOL_WINDOWS[2 * col], POOL_WINDOWS[2 * col + 1]
        s_lo = shifted(0, col)
        for d in range(1, w_lo):
            s_lo = s_lo + shifted(d, col)
        s_hi = s_lo
        for d in range(w_lo, w_hi):
            s_hi = s_hi + shifted(d, col)
        win_sum = jnp.where(low, s_lo, s_hi)
        count = jnp.minimum(t_seq, jnp.where(low, float(w_lo), float(w_hi)))
        mixed.append(win_sum / count - p[:, col * LANES:(col + 1) * LANES])
    mixed = jnp.concatenate(mixed, axis=1)
    ext_ref[0:POOL_HALO, :] = ext_ref[tt:tt + POOL_HALO, :]
    pool = _dot(mixed.astype(MXU_DTYPE), wpool_ref[...]) * pscale_ref[...]
    pool_ref[0] = pool.astype(pool_ref.dtype)

    ang = pos_ref[0].astype(F32) * invf_ref[...]
    cos = jnp.cos(ang)
    sin = jnp.sin(ang)

    o1 = D_POOL
    o2 = o1 + Q_LORA
    o3 = o2 + KV_LORA
    cqn = _rms_norm(z[:, o1:o2], qg_ref[...])
    qq = _dot(cqn.astype(MXU_DTYPE), wuq_ref[...])
    n_nope = N_HEADS * QK_NOPE
    n_rope = N_HEADS * QK_ROPE
    reps = n_rope // LANES
    cos_r = jnp.concatenate([cos] * reps, axis=1)
    sin_r = jnp.concatenate([sin] * reps, axis=1)
    q_rope = qq[:, n_nope:n_nope + n_rope] * cos_r + qq[:, n_nope + n_rope:] * sin_r
    scale = QK_DIM ** -0.5
    for hd in range(N_HEADS):
        qh = jnp.concatenate([qq[:, hd * QK_NOPE:(hd + 1) * QK_NOPE],
                              q_rope[:, hd * QK_ROPE:(hd + 1) * QK_ROPE]], axis=1) * scale
        q_ref[0, hd] = qh.astype(q_ref.dtype)

    ckvn = _rms_norm(z[:, o2:o3], kvg_ref[...])
    kv = _dot(ckvn.astype(MXU_DTYPE), wukv_ref[...])
    kk = z[:, o3:] * jnp.where(low, cos, sin)
    k_rope = (kk + pltpu.roll(kk, QK_ROPE, 1))[:, :QK_ROPE]
    per_head = QK_NOPE + V_HEAD
    for hd in range(N_HEADS):
        kh = jnp.concatenate([kv[:, hd * per_head:hd * per_head + QK_NOPE], k_rope], axis=1)
        k_ref[0, hd] = kh.astype(k_ref.dtype)
        v_ref[0, hd] = kv[:, hd * per_head + QK_NOPE:(hd + 1) * per_head].astype(v_ref.dtype)


def _rotate_half_cols(w):
    half = w.shape[-1] // 2
    return jnp.concatenate([-w[..., half:], w[..., :half]], axis=-1)


def _inproj(x, mod, positions, w_in, pool_w, pool_scale, q_norm_g, w_uq, kv_norm_g, w_ukv):
    bsz, seq, d = x.shape
    tt = min(TOKEN_TILE, seq)
    o3 = D_POOL + Q_LORA + KV_LORA
    w_in_x = jnp.concatenate([w_in, _rotate_half_cols(w_in[:, o3:])], axis=1).astype(MXU_DTYPE)
    n_groups = len(POOL_WINDOWS)
    w_pool_bd = jnp.zeros((D_POOL, D_POOL), F32)
    for g in range(n_groups):
        w_pool_bd = w_pool_bd.at[g * POOL_GROUP:(g + 1) * POOL_GROUP,
                                 g * POOL_GROUP:(g + 1) * POOL_GROUP].set(pool_w[g])
    w_pool_bd = w_pool_bd.astype(MXU_DTYPE)
    w_uq3 = w_uq.reshape(Q_LORA, N_HEADS, QK_DIM)
    w_q_rope = w_uq3[:, :, QK_NOPE:]
    w_uq_x = jnp.concatenate([w_uq3[:, :, :QK_NOPE].reshape(Q_LORA, N_HEADS * QK_NOPE),
                              w_q_rope.reshape(Q_LORA, N_HEADS * QK_ROPE),
                              _rotate_half_cols(w_q_rope).reshape(Q_LORA, N_HEADS * QK_ROPE)],
                             axis=1).astype(MXU_DTYPE)
    inv_freq = ROPE_THETA ** (-jnp.arange(0, QK_ROPE, 2, dtype=F32) / QK_ROPE)
    invf = jnp.tile(inv_freq, LANES // inv_freq.shape[0]).reshape(1, LANES)
    full = lambda shape: pl.BlockSpec(shape, lambda b, j: (0,) * len(shape))
    return pl.pallas_call(
        _inproj_kernel,
        grid=(bsz, seq // tt),
        in_specs=[pl.BlockSpec((1, tt, d), lambda b, j: (b, j, 0)),
                  pl.BlockSpec((1, 6, d), lambda b, j: (b, 0, 0)),
                  pl.BlockSpec((1, tt, 1), lambda b, j: (b, j, 0)),
                  full((1, LANES)),
                  full(w_in_x.shape),
                  full((D_POOL, D_POOL)),
                  full((1, D_POOL)),
                  full((1, Q_LORA)),
                  full(w_uq_x.shape),
                  full((1, KV_LORA)),
                  full(w_ukv.shape)],
        out_specs=[pl.BlockSpec((1, tt, D_POOL), lambda b, j: (b, j, 0)),
                   pl.BlockSpec((1, N_HEADS, tt, QK_DIM), lambda b, j: (b, 0, j, 0)),
                   pl.BlockSpec((1, N_HEADS, tt, QK_DIM), lambda b, j: (b, 0, j, 0)),
                   pl.BlockSpec((1, N_HEADS, tt, V_HEAD), lambda b, j: (b, 0, j, 0))],
        out_shape=[jax.ShapeDtypeStruct((bsz, seq, D_POOL), MXU_DTYPE),
                   jax.ShapeDtypeStruct((bsz, N_HEADS, seq, QK_DIM), MXU_DTYPE),
                   jax.ShapeDtypeStruct((bsz, N_HEADS, seq, QK_DIM), MXU_DTYPE),
                   jax.ShapeDtypeStruct((bsz, N_HEADS, seq, V_HEAD), MXU_DTYPE)],
        scratch_shapes=[pltpu.VMEM((tt + POOL_HALO, D_POOL), F32)],
        compiler_params=pltpu.CompilerParams(dimension_semantics=("arbitrary", "arbitrary"),
                                             vmem_limit_bytes=VMEM_LIMIT_BYTES),
        name="in_proj",
    )(x, mod.reshape(bsz, 6, d), positions.reshape(bsz, seq, 1), invf, w_in_x, w_pool_bd,
      pool_scale.reshape(1, D_POOL), q_norm_g.reshape(1, Q_LORA), w_uq_x,
      kv_norm_g.reshape(1, KV_LORA), w_ukv.astype(MXU_DTYPE))


def _attn_kernel(q_ref, k_ref, v_ref, o_ref, m_ref, l_ref, acc_ref, *, blk):
    seq = q_ref.shape[2]

    def q_block(qi, carry):
        q0 = pl.multiple_of(qi * blk, blk)
        q = q_ref[0, 0, pl.ds(q0, blk), :]
        m_ref[...] = jnp.full(m_ref.shape, -jnp.inf, F32)
        l_ref[...] = jnp.zeros(l_ref.shape, F32)
        acc_ref[...] = jnp.zeros(acc_ref.shape, F32)

        def kv_block(kj, diagonal):
            k0 = pl.multiple_of(kj * blk, blk)
            k = k_ref[0, 0, pl.ds(k0, blk), :]
            v = v_ref[0, 0, pl.ds(k0, blk), :]
            s = _dot_nt(q, k)
            if diagonal:
                row = lax.broadcasted_iota(jnp.int32, (blk, blk), 0)
                col = lax.broadcasted_iota(jnp.int32, (blk, blk), 1)
                s = jnp.where(row >= col, s, -jnp.inf)
            m_prev = m_ref[:, 0:1]
            m_new = jnp.maximum(m_prev, jnp.max(s, axis=1, keepdims=True))
            alpha = jnp.exp(m_prev - m_new)
            p = jnp.exp(s - m_new)
            l_ref[...] = alpha * l_ref[...] + jnp.sum(p, axis=1, keepdims=True)
            acc_ref[...] = alpha * acc_ref[...] + _dot(p.astype(MXU_DTYPE), v)
            m_ref[...] = jnp.broadcast_to(m_new, m_ref.shape)

        def full_block(kj, c):
            kv_block(kj, False)
            return c

        lax.fori_loop(0, qi, full_block, 0)
        kv_block(qi, True)
        o_ref[0, pl.ds(q0, blk), :] = (acc_ref[...] / l_ref[...]).astype(o_ref.dtype)
        return carry

    lax.fori_loop(0, seq // blk, q_block, 0)


def _attention(q, k, v):
    bsz, nh, seq, _ = q.shape
    blk = min(ATTN_BLOCK, seq)
    return pl.pallas_call(
        functools.partial(_attn_kernel, blk=blk),
        grid=(bsz, nh),
        in_specs=[pl.BlockSpec((1, 1, seq, QK_DIM), lambda b, h: (b, h, 0, 0)),
                  pl.BlockSpec((1, 1, seq, QK_DIM), lambda b, h: (b, h, 0, 0)),
                  pl.BlockSpec((1, 1, seq, V_HEAD), lambda b, h: (b, h, 0, 0))],
        out_specs=pl.BlockSpec((1, seq, V_HEAD), lambda b, h: (b, 0, h)),
        out_shape=jax.ShapeDtypeStruct((bsz, seq, nh * V_HEAD), MXU_DTYPE),
        scratch_shapes=[pltpu.VMEM((blk, LANES), F32),
                        pltpu.VMEM((blk, LANES), F32),
                        pltpu.VMEM((blk, V_HEAD), F32)],
        compiler_params=pltpu.CompilerParams(dimension_semantics=("arbitrary", "arbitrary"),
                                             vmem_limit_bytes=VMEM_LIMIT_BYTES),
        name="mla_attention",
    )(q, k, v)


def _outproj_kernel(x_ref, pool_ref, att_ref, mod_ref, wout_ref, g_ref, b_ref, x1_ref, h2_ref):
    mix = _dot(pool_ref[0], wout_ref[0:D_POOL, :]) + _dot(att_ref[0], wout_ref[D_POOL:, :])
    g1 = mod_ref[0, 2:3, :]
    sh2 = mod_ref[0, 3:4, :]
    sc2 = mod_ref[0, 4:5, :]
    x1 = _layer_norm(ALPHA * x_ref[0] + g1 * mix, g_ref[...], b_ref[...])
    x1_ref[0] = x1
    h2_ref[0] = (x1 * (1.0 + sc2) + sh2).astype(h2_ref.dtype)


def _outproj(x, pool, att, mod, w_out, ln_g, ln_b):
    bsz, seq, d = x.shape
    tt = min(TOKEN_TILE, seq)
    tile = lambda n: pl.BlockSpec((1, tt, n), lambda b, j: (b, j, 0))
    full = lambda shape: pl.BlockSpec(shape, lambda b, j: (0,) * len(shape))
    return pl.pallas_call(
        _outproj_kernel,
        grid=(bsz, seq // tt),
        in_specs=[tile(d), tile(D_POOL), tile(D_ATT),
                  pl.BlockSpec((1, 6, d), lambda b, j: (b, 0, 0)),
                  full(w_out.shape), full((1, d)), full((1, d))],
        out_specs=[tile(d), tile(d)],
        out_shape=[jax.ShapeDtypeStruct((bsz, seq, d), F32),
                   jax.ShapeDtypeStruct((bsz, seq, d), MXU_DTYPE)],
        compiler_params=pltpu.CompilerParams(dimension_semantics=("arbitrary", "arbitrary"),
                                             vmem_limit_bytes=VMEM_LIMIT_BYTES),
        name="out_proj_ln1",
    )(x, pool, att, mod.reshape(bsz, 6, d), w_out.astype(MXU_DTYPE), ln_g.reshape(1, d),
      ln_b.reshape(1, d))


def _extract_topk(s, k):
    n = s.shape[0]
    iota = lax.broadcasted_iota(jnp.int32, s.shape, 0)
    rank = jnp.full(s.shape, NOT_SELECTED, F32)
    vals = []
    for r in range(k):
        m = jnp.max(s, axis=0, keepdims=True)
        first = jnp.min(jnp.where(s == m, iota, n), axis=0, keepdims=True)
        sel = iota == first
        rank = jnp.where(sel, float(r + 1), rank)
        s = jnp.where(sel, -jnp.inf, s)
        vals.append(m)
    return rank, jnp.concatenate(vals, axis=0)


def _peer_kernel(h2_ref, x1_ref, mod_ref, wq_ref, keys_ref, u_ref, vt_ref, g_ref, b_ref, o_ref,
                 qt_ref, lim1_ref, e1_ref, r2_ref, e2_ref, w_ref, acc_ref):
    c = pl.program_id(1)
    tt = h2_ref.shape[0]
    k = PEER_TOPK

    @pl.when(c == 0)
    def _():
        acc_ref[...] = jnp.zeros(acc_ref.shape, F32)
        qt_ref[...] = _dot_nt(wq_ref[...], h2_ref[...]).astype(qt_ref.dtype)

        def head(hd, carry):
            r0 = pl.multiple_of(hd * 2 * PEER_HALF, 2 * PEER_HALF)
            s1 = _dot(keys_ref[2 * hd], qt_ref[pl.ds(r0, PEER_HALF), :])
            s2 = _dot(keys_ref[2 * hd + 1], qt_ref[pl.ds(r0 + PEER_HALF, PEER_HALF), :])
            rank1, a = _extract_topk(s1, k)
            rank2, b = _extract_topk(s2, k)
            blocks = [a[0:1] + b, a[1:2] + b[0:8]]
            blocks += [a[r:r + 1] + b[0:8] for r in range(2, 8)]
            blocks += [a[8:16] + b[0:1]]
            cand = jnp.concatenate(blocks, axis=0)
            crank, _ = _extract_topk(cand, k)
            sel = crank < NOT_SELECTED
            self_f = jnp.where(sel, 1.0, 0.0)
            row_len = [jnp.sum(self_f[0:16], axis=0, keepdims=True),
                       jnp.sum(self_f[16:24], axis=0, keepdims=True)]
            row_len += [jnp.sum(self_f[24 + 8 * r:32 + 8 * r], axis=0, keepdims=True) for r in range(6)]
            row_len += [self_f[72 + r:73 + r] for r in range(8)]
            top = a[0:1] + b[0:1]
            z = jnp.sum(jnp.where(sel, jnp.exp(cand - top), 0.0), axis=0, keepdims=True)
            lim1 = jnp.zeros(s1.shape, F32)
            for r in range(k):
                lim1 = jnp.where(rank1 == float(r + 1), row_len[r], lim1)
            lim1_ref[hd] = lim1
            e1_ref[hd] = jnp.exp(s1 - a[0:1]) / z
            r2_ref[hd] = rank2
            e2_ref[hd] = jnp.exp(s2 - b[0:1])
            return carry

        lax.fori_loop(0, PEER_HEADS, head, 0)

    a_t = _dot_nt(u_ref[...], h2_ref[...])
    rows = u_ref.shape[0] // PEER_NKEYS
    for ii in range(rows):
        i = c * rows + ii
        gate = None
        for hd in range(PEER_HEADS):
            lim = lim1_ref[hd, pl.ds(i, 1), :]
            e1r = e1_ref[hd, pl.ds(i, 1), :]
            term = jnp.where(r2_ref[hd] <= lim, e2_ref[hd] * e1r, 0.0)
            gate = term if gate is None else gate + term
        a_blk = a_t[ii * PEER_NKEYS:(ii + 1) * PEER_NKEYS, :]
        gelu = 0.5 * a_blk * (1.0 + lax.erf(a_blk * (2.0 ** -0.5)))
        w_ref[ii * PEER_NKEYS:(ii + 1) * PEER_NKEYS, :] = (gate * gelu).astype(w_ref.dtype)
    acc_ref[...] += _dot(vt_ref[...], w_ref[...])

    @pl.when(c == pl.num_programs(1) - 1)
    def _():
        ffn = acc_ref[...].T
        g2 = mod_ref[0, 5:6, :]
        o_ref[...] = _layer_norm(ALPHA * x1_ref[...] + g2 * ffn, g_ref[...], b_ref[...])


def _peer(h2, x1, mod, w_peer_q, peer_keys, peer_u, peer_v, ln_g, ln_b, seq):
    n_tok, d = h2.shape
    bsz = n_tok // seq
    tt = min(TOKEN_TILE, seq)
    rows = PEER_ROWS_PER_CHUNK
    ec = rows * PEER_NKEYS
    n_exp = peer_u.shape[0]
    n_q = w_peer_q.shape[1]
    tiles_per_seq = seq // tt
    full = lambda shape: pl.BlockSpec(shape, lambda t, c: (0,) * len(shape))
    hk = (PEER_HEADS, PEER_NKEYS, tt)
    return pl.pallas_call(
        _peer_kernel,
        grid=(n_tok // tt, n_exp // ec),
        in_specs=[pl.BlockSpec((tt, d), lambda t, c: (t, 0)),
                  pl.BlockSpec((tt, d), lambda t, c: (t, 0)),
                  pl.BlockSpec((1, 6, d), lambda t, c: (t // tiles_per_seq, 0, 0)),
                  full((n_q, d)),
                  full((2 * PEER_HEADS, PEER_NKEYS, PEER_HALF)),
                  pl.BlockSpec((ec, d), lambda t, c: (c, 0)),
                  pl.BlockSpec((d, ec), lambda t, c: (0, c)),
                  full((1, d)), full((1, d))],
        out_specs=pl.BlockSpec((tt, d), lambda t, c: (t, 0)),
        out_shape=jax.ShapeDtypeStruct((n_tok, d), F32),
        scratch_shapes=[pltpu.VMEM((n_q, tt), MXU_DTYPE),
                        pltpu.VMEM(hk, F32), pltpu.VMEM(hk, F32),
                        pltpu.VMEM(hk, F32), pltpu.VMEM(hk, F32),
                        pltpu.VMEM((ec, tt), MXU_DTYPE),
                        pltpu.VMEM((d, tt), F32)],
        compiler_params=pltpu.CompilerParams(dimension_semantics=("arbitrary", "arbitrary"),
                                             vmem_limit_bytes=VMEM_LIMIT_BYTES),
        name="peer_ln2",
    )(h2, x1, mod.reshape(bsz, 6, d), w_peer_q.T.astype(MXU_DTYPE),
      peer_keys.reshape(2 * PEER_HEADS, PEER_NKEYS, PEER_HALF).astype(MXU_DTYPE),
      peer_u.astype(MXU_DTYPE), peer_v.T.astype(MXU_DTYPE), ln_g.reshape(1, d), ln_b.reshape(1, d))


def kernel(x, c, positions, w_ada, b_ada, w_in, pool_w, pool_scale, q_norm_g, w_uq, kv_norm_g,
           w_ukv, w_out, ln1_g, ln1_b, w_peer_q, peer_keys, peer_u, peer_v, ln2_g, ln2_b):
    bsz, seq, d = x.shape
    assert w_ada.shape[0] == DEPTH
    for l in range(DEPTH):
        mod = _adaln(c, w_ada[l], b_ada[l])
        pool, q, k, v = _inproj(x, mod, positions, w_in[l], pool_w[l], pool_scale[l], q_norm_g[l],
                                w_uq[l], kv_norm_g[l], w_ukv[l])
        att = _attention(q, k, v)
        x1, h2 = _outproj(x, pool, att, mod, w_out[l], ln1_g[l], ln1_b[l])
        x = _peer(h2.reshape(bsz * seq, d), x1.reshape(bsz * seq, d), mod, w_peer_q[l],
                  peer_keys[l], peer_u[l], peer_v[l], ln2_g[l], ln2_b[l], seq).reshape(bsz, seq, d)
    return x
```

```python
import functools

import jax
import jax.numpy as jnp
from jax import lax
from jax.experimental import pallas as pl
from jax.experimental.pallas import tpu as pltpu

POOL_WINDOWS = (2, 4, 8, 16)
POOL_GROUP = 64
D_POOL = POOL_GROUP * len(POOL_WINDOWS)
V_HEAD = 128
QK_NOPE = 128
QK_ROPE = 64
QK_DIM = QK_NOPE + QK_ROPE
N_HEADS = 6
D_ATT = N_HEADS * V_HEAD
Q_LORA = 384
KV_LORA = 256
ROPE_THETA = 10000.0
PEER_HEADS = 8
PEER_NKEYS = 128
PEER_HALF = 128
PEER_TOPK = 16
DEPTH = 1
ALPHA = (2.0 * DEPTH) ** 0.25
LN_EPS = 1e-5
RMS_EPS = 1e-6
LOG2_E = 1.4426950408889634

LANES = 128
SUBLANES = 8
PACKED_SUBLANES = 16
VMEM_LIMIT_BYTES = 56 * 1024 * 1024

MXU_DTYPE = jnp.bfloat16
GATE_DTYPE = jnp.float32
F32 = jnp.float32
NOT_SELECTED = 1.0e4

POOL_HALO = 16
TOKEN_TILE = 512
ATTN_Q_BLOCK = 1024
ATTN_CHUNK = 256
PEER_TOKEN_TILE = 512
PEER_ROWS_PER_CHUNK = 16
PEER_ROWS_PER_CHAIN = 4


def _dot(a, b):
    return jnp.dot(a, b, preferred_element_type=F32)


def _dot_nt(a, b):
    return lax.dot_general(a, b, (((1,), (1,)), ((), ())), preferred_element_type=F32)


def _layer_norm(y, g, b):
    mu = jnp.mean(y, axis=-1, keepdims=True)
    d = y - mu
    var = jnp.mean(d * d, axis=-1, keepdims=True)
    return d * lax.rsqrt(var + LN_EPS) * g + b


def _rms_norm(x, g):
    ms = jnp.mean(x * x, axis=-1, keepdims=True)
    return x * lax.rsqrt(ms + RMS_EPS) * g


def _adaln_kernel(c_ref, w_ref, b_ref, o_ref):
    c = c_ref[...]
    c_act = c * jax.nn.sigmoid(c)
    o_ref[...] = _dot(c_act.astype(MXU_DTYPE), w_ref[...].astype(MXU_DTYPE)) + b_ref[...]


def _adaln(c, w_ada, b_ada):
    bsz, d = c.shape
    n = w_ada.shape[1]
    blk = d
    return pl.pallas_call(
        _adaln_kernel,
        grid=(n // blk,),
        in_specs=[pl.BlockSpec((bsz, d), lambda i: (0, 0)),
                  pl.BlockSpec((d, blk), lambda i: (0, i)),
                  pl.BlockSpec((1, blk), lambda i: (0, i))],
        out_specs=pl.BlockSpec((bsz, blk), lambda i: (0, i)),
        out_shape=jax.ShapeDtypeStruct((bsz, n), F32),
        compiler_params=pltpu.CompilerParams(dimension_semantics=("arbitrary",),
                                             vmem_limit_bytes=VMEM_LIMIT_BYTES),
        name="adaln_mod",
    )(c, w_ada, b_ada.reshape(1, n))


def _inproj_kernel(x_ref, mod_ref, pos_ref, invf_ref, win_ref, wpool_ref, pscale_ref, qg_ref,
                   wuqt_ref, kvg_ref, wuk_ref, wuvt_ref, pool_ref, qt_ref, k_ref, vt_ref, ext_ref):
    j = pl.program_id(1)
    tt = x_ref.shape[1]
    x = x_ref[0]
    sh1 = mod_ref[0, 0:1, :]
    sc1 = mod_ref[0, 1:2, :]
    h = x * (1.0 + sc1) + sh1
    z = _dot(h.astype(MXU_DTYPE), win_ref[...])

    @pl.when(j == 0)
    def _():
        ext_ref[0:POOL_HALO, :] = jnp.zeros((POOL_HALO, D_POOL), F32)

    p = z[:, :D_POOL]
    ext_ref[POOL_HALO:POOL_HALO + tt, :] = p

    def shifted(d, col):
        return ext_ref[pl.ds(POOL_HALO - d, tt), col * LANES:(col + 1) * LANES]

    lane = lax.broadcasted_iota(jnp.int32, (tt, LANES), 1)
    low = lane < POOL_GROUP
    t_seq = (j * tt + lax.broadcasted_iota(jnp.int32, (tt, LANES), 0) + 1).astype(F32)
    mixed = []
    for col in range(2):
        w_lo, w_hi = POOL_WINDOWS[2 * col], POOL_WINDOWS[2 * col + 1]
        s_lo = shifted(0, col)
        for d in range(1, w_lo):
            s_lo = s_lo + shifted(d, col)
        s_hi = s_lo
        for d in range(w_lo, w_hi):
            s_hi = s_hi + shifted(d, col)
        win_sum = jnp.where(low, s_lo, s_hi)
        count = jnp.minimum(t_seq, jnp.where(low, float(w_lo), float(w_hi)))
        mixed.append(win_sum / count - p[:, col * LANES:(col + 1) * LANES])
    mixed = jnp.concatenate(mixed, axis=1)
    ext_ref[0:POOL_HALO, :] = ext_ref[tt:tt + POOL_HALO, :]
    pool = _dot(mixed.astype(MXU_DTYPE), wpool_ref[...]) * pscale_ref[...]
    pool_ref[0] = pool.astype(pool_ref.dtype)

    ang_t = invf_ref[...] * pos_ref[0].astype(F32)
    cos_t = jnp.cos(ang_t)
    sin_t = jnp.sin(ang_t)
    cos_t = jnp.concatenate([cos_t, cos_t], axis=0)
    sin_t = jnp.concatenate([sin_t, sin_t], axis=0)

    o1 = D_POOL
    o2 = o1 + Q_LORA
    o3 = o2 + KV_LORA
    cqn = _rms_norm(z[:, o1:o2], qg_ref[...]).astype(MXU_DTYPE)
    qq_t = _dot_nt(wuqt_ref[...], cqn)
    n_nope = N_HEADS * QK_NOPE
    n_rope = N_HEADS * QK_ROPE
    scale = QK_DIM ** -0.5 * LOG2_E
    cw = qt_ref.shape[4]
    for hd in range(N_HEADS):
        r0 = n_nope + hd * QK_ROPE
        rope_t = qq_t[r0:r0 + QK_ROPE] * cos_t + qq_t[r0 + n_rope:r0 + n_rope + QK_ROPE] * sin_t
        qh_t = (jnp.concatenate([qq_t[hd * QK_NOPE:(hd + 1) * QK_NOPE], rope_t], axis=0)
                * scale).astype(qt_ref.dtype)
        for cc in range(tt // cw):
            qt_ref[0, hd, cc] = qh_t[:, cc * cw:(cc + 1) * cw]

    ckvn = _rms_norm(z[:, o2:o3], kvg_ref[...]).astype(MXU_DTYPE)
    k_nope = _dot(ckvn, wuk_ref[...])
    v_t = _dot_nt(wuvt_ref[...], ckvn)
    cs = jnp.concatenate([cos_t, sin_t], axis=0).T
    kk = z[:, o3:] * cs
    k_rope = (kk + pltpu.roll(kk, QK_ROPE, 1))[:, :QK_ROPE]
    for hd in range(N_HEADS):
        kh = jnp.concatenate([k_nope[:, hd * QK_NOPE:(hd + 1) * QK_NOPE], k_rope], axis=1)
        k_ref[0, hd] = kh.astype(k_ref.dtype)
        vt_ref[0, hd, 0] = v_t[hd * V_HEAD:(hd + 1) * V_HEAD].astype(vt_ref.dtype)


def _rotate_half_cols(w):
    half = w.shape[-1] // 2
    return jnp.concatenate([-w[..., half:], w[..., :half]], axis=-1)


def _inproj(x, mod, positions, w_in, pool_w, pool_scale, q_norm_g, w_uq, kv_norm_g, w_ukv, tt, cw):
    bsz, seq, d = x.shape
    o3 = D_POOL + Q_LORA + KV_LORA
    w_in_x = jnp.concatenate([w_in, _rotate_half_cols(w_in[:, o3:])], axis=1).astype(MXU_DTYPE)
    n_groups = len(POOL_WINDOWS)
    w_pool_bd = jnp.zeros((D_POOL, D_POOL), F32)
    for g in range(n_groups):
        w_pool_bd = w_pool_bd.at[g * POOL_GROUP:(g + 1) * POOL_GROUP,
                                 g * POOL_GROUP:(g + 1) * POOL_GROUP].set(pool_w[g])
    w_pool_bd = w_pool_bd.astype(MXU_DTYPE)
    w_uq3 = w_uq.reshape(Q_LORA, N_HEADS, QK_DIM)
    w_q_rope = w_uq3[:, :, QK_NOPE:]
    w_uq_t = jnp.concatenate([w_uq3[:, :, :QK_NOPE].reshape(Q_LORA, N_HEADS * QK_NOPE),
                              w_q_rope.reshape(Q_LORA, N_HEADS * QK_ROPE),
                              _rotate_half_cols(w_q_rope).reshape(Q_LORA, N_HEADS * QK_ROPE)],
                             axis=1).T.astype(MXU_DTYPE)
    w_ukv3 = w_ukv.reshape(KV_LORA, N_HEADS, QK_NOPE + V_HEAD)
    w_uk = w_ukv3[:, :, :QK_NOPE].reshape(KV_LORA, N_HEADS * QK_NOPE).astype(MXU_DTYPE)
    w_uv_t = w_ukv3[:, :, QK_NOPE:].reshape(KV_LORA, N_HEADS * V_HEAD).T.astype(MXU_DTYPE)
    inv_freq = ROPE_THETA ** (-jnp.arange(0, QK_ROPE, 2, dtype=F32) / QK_ROPE)
    invf = inv_freq.reshape(QK_ROPE // 2, 1)
    full = lambda shape: pl.BlockSpec(shape, lambda b, j: (0,) * len(shape))
    return pl.pallas_call(
        _inproj_kernel,
        grid=(bsz, seq // tt),
        in_specs=[pl.BlockSpec((1, tt, d), lambda b, j: (b, j, 0)),
                  pl.BlockSpec((1, 6, d), lambda b, j: (b, 0, 0)),
                  pl.BlockSpec((1, 1, tt), lambda b, j: (b, 0, j)),
                  full(invf.shape),
                  full(w_in_x.shape),
                  full((D_POOL, D_POOL)),
                  full((1, D_POOL)),
                  full((1, Q_LORA)),
                  full(w_uq_t.shape),
                  full((1, KV_LORA)),
                  full(w_uk.shape),
                  full(w_uv_t.shape)],
        out_specs=[pl.BlockSpec((1, tt, D_POOL), lambda b, j: (b, j, 0)),
                   pl.BlockSpec((1, N_HEADS, tt // cw, QK_DIM, cw), lambda b, j: (b, 0, j, 0, 0)),
                   pl.BlockSpec((1, N_HEADS, tt, QK_DIM), lambda b, j: (b, 0, j, 0)),
                   pl.BlockSpec((1, N_HEADS, 1, V_HEAD, tt), lambda b, j: (b, 0, j, 0, 0))],
        out_shape=[jax.ShapeDtypeStruct((bsz, seq, D_POOL), MXU_DTYPE),
                   jax.ShapeDtypeStruct((bsz, N_HEADS, seq // cw, QK_DIM, cw), MXU_DTYPE),
                   jax.ShapeDtypeStruct((bsz, N_HEADS, seq, QK_DIM), MXU_DTYPE),
                   jax.ShapeDtypeStruct((bsz, N_HEADS, seq // tt, V_HEAD, tt), MXU_DTYPE)],
        scratch_shapes=[pltpu.VMEM((tt + POOL_HALO, D_POOL), F32)],
        compiler_params=pltpu.CompilerParams(dimension_semantics=("arbitrary", "arbitrary"),
                                             vmem_limit_bytes=VMEM_LIMIT_BYTES),
        name="in_proj",
    )(x, mod.reshape(bsz, 6, d), positions.reshape(bsz, 1, seq), invf, w_in_x, w_pool_bd,
      pool_scale.reshape(1, D_POOL), q_norm_g.reshape(1, Q_LORA), w_uq_t,
      kv_norm_g.reshape(1, KV_LORA), w_uk, w_uv_t)


def _attn_kernel(qt_ref, k_ref, vt_ref, o_ref, sa_ref, sb_ref, m_ref, l_ref, acc_ref, *, tq, tk, cw):
    seq = k_ref.shape[2]
    n_chunks = tq // cw
    assert tq == 2 * tk

    def q_block(qi, carry):
        q0 = pl.multiple_of(qi * tq, tq)
        m_ref[...] = jnp.full(m_ref.shape, -jnp.inf, F32)
        l_ref[...] = jnp.zeros(l_ref.shape, F32)
        acc_ref[...] = jnp.zeros(acc_ref.shape, F32)

        def active_chunks(rel):
            return [c for c in range(n_chunks) if rel is None or (c + 1) * cw > rel * tk]

        def scores_into(buf_ref, kj, rel):
            k0 = pl.multiple_of(kj * tk, tk)
            k = k_ref[0, 0, pl.ds(k0, tk), :]
            for c in active_chunks(rel):
                buf_ref[c] = _dot(k, qt_ref[0, 0, qi * n_chunks + c])

        def consume(buf_ref, kj, rel):
            vt = vt_ref[0, 0, kj]
            probs, alphas = {}, {}
            for c in active_chunks(rel):
                cols = slice(c * cw, (c + 1) * cw)
                s = buf_ref[c]
                if rel is not None and c * cw < (rel + 1) * tk:
                    kv_idx = rel * tk + lax.broadcasted_iota(jnp.int32, (tk, cw), 0)
                    q_idx = c * cw + lax.broadcasted_iota(jnp.int32, (tk, cw), 1)
                    s = jnp.where(q_idx >= kv_idx, s, -jnp.inf)
                m_prev = m_ref[:, cols]
                m_new = jnp.maximum(m_prev, jnp.max(s, axis=0, keepdims=True))
                alphas[c] = jnp.exp2(m_prev - m_new)
                p = jnp.exp2(s - m_new)
                l_ref[:, cols] = alphas[c] * l_ref[:, cols] + jnp.sum(p, axis=0, keepdims=True)
                m_ref[:, cols] = m_new
                probs[c] = p.astype(MXU_DTYPE)
            for c in active_chunks(rel):
                cols = slice(c * cw, (c + 1) * cw)
                acc_ref[:, cols] = alphas[c] * acc_ref[:, cols] + _dot(vt, probs[c])

        def pair(jj, c):
            scores_into(sb_ref, 2 * jj + 1, None)
            consume(sa_ref, 2 * jj, None)
            scores_into(sa_ref, 2 * jj + 2, None)
            consume(sb_ref, 2 * jj + 1, None)
            return c

        scores_into(sa_ref, 0, None)
        lax.fori_loop(0, qi, pair, 0)
        scores_into(sb_ref, 2 * qi + 1, 1)
        consume(sa_ref, 2 * qi, 0)
        consume(sb_ref, 2 * qi + 1, 1)
        out_t = acc_ref[...] / l_ref[...]
        o_ref[0, pl.ds(q0, tq), :] = out_t.T.astype(o_ref.dtype)
        return carry

    lax.fori_loop(0, seq // tq, q_block, 0)


def _attention(qt, k, vt, tq, cw):
    bsz, nh, seq, _ = k.shape
    tk = vt.shape[4]
    return pl.pallas_call(
        functools.partial(_attn_kernel, tq=tq, tk=tk, cw=cw),
        grid=(bsz, nh),
        in_specs=[pl.BlockSpec((1, 1, seq // cw, QK_DIM, cw), lambda b, h: (b, h, 0, 0, 0)),
                  pl.BlockSpec((1, 1, seq, QK_DIM), lambda b, h: (b, h, 0, 0)),
                  pl.BlockSpec((1, 1, seq // tk, V_HEAD, tk), lambda b, h: (b, h, 0, 0, 0))],
        out_specs=pl.BlockSpec((1, seq, V_HEAD), lambda b, h: (b, 0, h)),
        out_shape=jax.ShapeDtypeStruct((bsz, seq, nh * V_HEAD), MXU_DTYPE),
        scratch_shapes=[pltpu.VMEM((tq // cw, tk, cw), F32),
                        pltpu.VMEM((tq // cw, tk, cw), F32),
                        pltpu.VMEM((1, tq), F32),
                        pltpu.VMEM((1, tq), F32),
                        pltpu.VMEM((V_HEAD, tq), F32)],
        compiler_params=pltpu.CompilerParams(dimension_semantics=("arbitrary", "arbitrary"),
                                             vmem_limit_bytes=VMEM_LIMIT_BYTES),
        name="mla_attention",
    )(qt, k, vt)


def _outproj_kernel(x_ref, pool_ref, att_ref, mod_ref, wout_ref, g_ref, b_ref, x1_ref, h2t_ref):
    mix = _dot(pool_ref[0], wout_ref[0:D_POOL, :]) + _dot(att_ref[0], wout_ref[D_POOL:, :])
    g1 = mod_ref[0, 2:3, :]
    sh2 = mod_ref[0, 3:4, :]
    sc2 = mod_ref[0, 4:5, :]
    x1 = _layer_norm(ALPHA * x_ref[0] + g1 * mix, g_ref[...], b_ref[...])
    x1_ref[0] = x1
    h2t_ref[0] = (x1 * (1.0 + sc2) + sh2).T.astype(h2t_ref.dtype)


def _outproj(x, pool, att, mod, w_out, ln_g, ln_b, tt):
    bsz, seq, d = x.shape
    tile = lambda n: pl.BlockSpec((1, tt, n), lambda b, j: (b, j, 0))
    full = lambda shape: pl.BlockSpec(shape, lambda b, j: (0,) * len(shape))
    return pl.pallas_call(
        _outproj_kernel,
        grid=(bsz, seq // tt),
        in_specs=[tile(d), tile(D_POOL), tile(D_ATT),
                  pl.BlockSpec((1, 6, d), lambda b, j: (b, 0, 0)),
                  full(w_out.shape), full((1, d)), full((1, d))],
        out_specs=[tile(d), pl.BlockSpec((1, d, tt), lambda b, j: (b, 0, j))],
        out_shape=[jax.ShapeDtypeStruct((bsz, seq, d), F32),
                   jax.ShapeDtypeStruct((bsz, d, seq), MXU_DTYPE)],
        compiler_params=pltpu.CompilerParams(dimension_semantics=("arbitrary", "arbitrary"),
                                             vmem_limit_bytes=VMEM_LIMIT_BYTES),
        name="out_proj_ln1",
    )(x, pool, att, mod.reshape(bsz, 6, d), w_out.astype(MXU_DTYPE), ln_g.reshape(1, d),
      ln_b.reshape(1, d))


def _extract_topk(s, k):
    n = s.shape[0]
    iota = lax.broadcasted_iota(jnp.int32, s.shape, 0).astype(F32)
    rank = jnp.full(s.shape, NOT_SELECTED, F32)
    vals = []
    sub = lax.broadcasted_iota(jnp.int32, (SUBLANES, s.shape[1]), 0).astype(F32)
    for r in range(k):
        nodes = [(s[g:g + SUBLANES], float(g)) for g in range(0, n, SUBLANES)]
        while len(nodes) > 1:
            merged = []
            for a in range(0, len(nodes) - 1, 2):
                (va, ga), (vb, gb) = nodes[a], nodes[a + 1]
                take_b = vb > va
                merged.append((jnp.where(take_b, vb, va), jnp.where(take_b, gb, ga)))
            if len(nodes) % 2:
                merged.append(nodes[-1])
            nodes = merged
        v8, g8 = nodes[0]
        m = jnp.max(v8, axis=0, keepdims=True)
        first = jnp.min(jnp.where(v8 == m, g8 + sub, float(n)), axis=0, keepdims=True)
        sel = iota == first
        rank = jnp.where(sel, float(r + 1), rank)
        s = jnp.where(sel, -jnp.inf, s)
        vals.append(m)
    return rank, jnp.concatenate(vals, axis=0)


def _peer_kernel(h2t_ref, x1_ref, mod_ref, wq_ref, keys_ref, u_ref, vt_ref, g_ref, b_ref, o_ref,
                 qt_ref, lim1_ref, e1_ref, r2_ref, e2_ref, gate_ref, a0_ref, a1_ref, w0_ref, w1_ref,
                 acc_ref):
    a_refs = (a0_ref, a1_ref)
    w_refs = (w0_ref, w1_ref)
    c = pl.program_id(1)
    tt = h2t_ref.shape[2]
    k = PEER_TOPK
    slabs = PEER_NKEYS // PACKED_SUBLANES
    tiles = PEER_NKEYS // SUBLANES
    n_tb = tt // LANES

    @pl.when(c == 0)
    def _():
        acc_ref[...] = jnp.zeros(acc_ref.shape, F32)
        qt_ref[...] = _dot(wq_ref[...], h2t_ref[0]).astype(qt_ref.dtype)

        def head(hd, carry):
            r0 = pl.multiple_of(hd * 2 * PEER_HALF, 2 * PEER_HALF)
            s1 = _dot(keys_ref[2 * hd], qt_ref[pl.ds(r0, PEER_HALF), :])
            s2 = _dot(keys_ref[2 * hd + 1], qt_ref[pl.ds(r0 + PEER_HALF, PEER_HALF), :])
            rank1, a = _extract_topk(s1, k)
            rank2, b = _extract_topk(s2, k)
            blocks = [a[0:1] + b, a[1:2] + b[0:8]]
            blocks += [a[r:r + 1] + b[0:8] for r in range(2, 8)]
            blocks += [a[8:16] + b[0:1]]
            cand = jnp.concatenate(blocks, axis=0)
            crank, _ = _extract_topk(cand, k)
            sel = crank < NOT_SELECTED
            sel_f = jnp.where(sel, 1.0, 0.0)
            row_len = [jnp.sum(sel_f[0:16], axis=0, keepdims=True),
                       jnp.sum(sel_f[16:24], axis=0, keepdims=True)]
            row_len += [jnp.sum(sel_f[24 + 8 * r:32 + 8 * r], axis=0, keepdims=True) for r in range(6)]
            row_len += [sel_f[72 + r:73 + r] for r in range(8)]
            top = a[0:1] + b[0:1]
            z = jnp.sum(jnp.where(sel, jnp.exp(cand - top), 0.0), axis=0, keepdims=True)
            lim1 = jnp.zeros(s1.shape, F32)
            for r in range(k):
                lim1 = jnp.where(rank1 == float(r + 1), row_len[r], lim1)
            e1 = jnp.exp(s1 - a[0:1]) / z
            r2 = rank2.astype(GATE_DTYPE)
            e2 = jnp.exp(s2 - b[0:1]).astype(GATE_DTYPE)
            for tb in range(n_tb):
                ls = slice(tb * LANES, (tb + 1) * LANES)
                lim1_ref[hd, tb] = lim1[:, ls].reshape(tiles, SUBLANES, LANES)
                e1_ref[hd, tb] = e1[:, ls].reshape(tiles, SUBLANES, LANES)
                r2_ref[hd, tb] = r2[:, ls].reshape(slabs, PACKED_SUBLANES, LANES)
                e2_ref[hd, tb] = e2[:, ls].reshape(slabs, PACKED_SUBLANES, LANES)
            return carry

        lax.fori_loop(0, PEER_HEADS, head, 0)

    rows = u_ref.shape[0] // PEER_NKEYS
    row_tiles = rows // SUBLANES
    half = SUBLANES // 2

    def gate_block(idx, carry):
        tile = idx // n_tb
        tb = idx % n_tb
        i_tile = c * row_tiles + tile
        for hf in range(2):
            accs = [[None] * slabs for _ in range(half)]
            for hd in range(PEER_HEADS):
                lim_t = lim1_ref[hd, tb, i_tile]
                e1_t = e1_ref[hd, tb, i_tile]
                lims = [jnp.broadcast_to(lim_t[hf * half + ii:hf * half + ii + 1],
                                         (PACKED_SUBLANES, LANES)).astype(GATE_DTYPE) for ii in range(half)]
                e1rs = [jnp.broadcast_to(e1_t[hf * half + ii:hf * half + ii + 1],
                                         (PACKED_SUBLANES, LANES)).astype(GATE_DTYPE) for ii in range(half)]
                for sl in range(slabs):
                    r2 = r2_ref[hd, tb, sl]
                    e2 = e2_ref[hd, tb, sl]
                    for ii in range(half):
                        term = jnp.where(r2 <= lims[ii], e2 * e1rs[ii], jnp.zeros((), GATE_DTYPE))
                        accs[ii][sl] = term if accs[ii][sl] is None else accs[ii][sl] + term
            for ii in range(half):
                for sl in range(slabs):
                    gate_ref[tb, (tile * SUBLANES + hf * half + ii) * slabs + sl] = accs[ii][sl]
        return carry

    lax.fori_loop(0, row_tiles * n_tb, gate_block, 0)

    h2t = h2t_ref[0]
    chain_rows = PEER_ROWS_PER_CHAIN
    chain_e = chain_rows * PEER_NKEYS
    n_chains = rows // chain_rows

    def expert_pre(ch):
        a_refs[ch % 2][...] = _dot(u_ref[ch * chain_e:(ch + 1) * chain_e, :], h2t)

    def gated(ch):
        for tb in range(n_tb):
            ls = slice(tb * LANES, (tb + 1) * LANES)
            for sb in range(chain_rows * slabs):
                r0 = sb * PACKED_SUBLANES
                a_blk = a_refs[ch % 2][r0:r0 + PACKED_SUBLANES, ls]
                gelu = 0.5 * a_blk * (1.0 + lax.erf(a_blk * (2.0 ** -0.5)))
                w_blk = gate_ref[tb, ch * chain_rows * slabs + sb] * gelu.astype(GATE_DTYPE)
                w_refs[ch % 2][r0:r0 + PACKED_SUBLANES, ls] = w_blk.astype(MXU_DTYPE)

    def expert_post(ch):
        acc_ref[...] += _dot(vt_ref[:, ch * chain_e:(ch + 1) * chain_e], w_refs[ch % 2][...])

    expert_pre(0)
    for ch in range(n_chains):
        if ch > 0:
            expert_post(ch - 1)
        if ch + 1 < n_chains:
            expert_pre(ch + 1)
        gated(ch)
    expert_post(n_chains - 1)

    @pl.when(c == pl.num_programs(1) - 1)
    def _():
        ffn = acc_ref[...].T
        g2 = mod_ref[0, 5:6, :]
        o_ref[...] = _layer_norm(ALPHA * x1_ref[...] + g2 * ffn, g_ref[...], b_ref[...])


def _peer(h2t, x1, mod, w_peer_q, peer_keys, peer_u, peer_v, ln_g, ln_b, seq, tt):
    n_tok, d = x1.shape
    bsz = n_tok // seq
    rows = PEER_ROWS_PER_CHUNK
    ec = rows * PEER_NKEYS
    n_exp = peer_u.shape[0]
    n_q = w_peer_q.shape[1]
    tiles_per_seq = seq // tt
    full = lambda shape: pl.BlockSpec(shape, lambda t, c: (0,) * len(shape))
    assert rows % SUBLANES == 0 and tt % LANES == 0
    n_tb = tt // LANES
    hk = (PEER_HEADS, n_tb, PEER_NKEYS // SUBLANES, SUBLANES, LANES)
    hk_packed = (PEER_HEADS, n_tb, PEER_NKEYS // PACKED_SUBLANES, PACKED_SUBLANES, LANES)
    gate_shape = (n_tb, rows * PEER_NKEYS // PACKED_SUBLANES, PACKED_SUBLANES, LANES)
    return pl.pallas_call(
        _peer_kernel,
        grid=(n_tok // tt, n_exp // ec),
        in_specs=[pl.BlockSpec((1, d, tt), lambda t, c: (t // tiles_per_seq, 0, t % tiles_per_seq)),
                  pl.BlockSpec((tt, d), lambda t, c: (t, 0)),
                  pl.BlockSpec((1, 6, d), lambda t, c: (t // tiles_per_seq, 0, 0)),
                  full((n_q, d)),
                  full((2 * PEER_HEADS, PEER_NKEYS, PEER_HALF)),
                  pl.BlockSpec((ec, d), lambda t, c: (c, 0)),
                  pl.BlockSpec((d, ec), lambda t, c: (0, c)),
                  full((1, d)), full((1, d))],
        out_specs=pl.BlockSpec((tt, d), lambda t, c: (t, 0)),
        out_shape=jax.ShapeDtypeStruct((n_tok, d), F32),
        scratch_shapes=[pltpu.VMEM((n_q, tt), MXU_DTYPE),
                        pltpu.VMEM(hk, F32), pltpu.VMEM(hk, F32),
                        pltpu.VMEM(hk_packed, GATE_DTYPE), pltpu.VMEM(hk_packed, GATE_DTYPE),
                        pltpu.VMEM(gate_shape, GATE_DTYPE),
                        pltpu.VMEM((PEER_ROWS_PER_CHAIN * PEER_NKEYS, tt), F32),
                        pltpu.VMEM((PEER_ROWS_PER_CHAIN * PEER_NKEYS, tt), F32),
                        pltpu.VMEM((PEER_ROWS_PER_CHAIN * PEER_NKEYS, tt), MXU_DTYPE),
                        pltpu.VMEM((PEER_ROWS_PER_CHAIN * PEER_NKEYS, tt), MXU_DTYPE),
                        pltpu.VMEM((d, tt), F32)],
        compiler_params=pltpu.CompilerParams(dimension_semantics=("arbitrary", "arbitrary"),
                                             vmem_limit_bytes=VMEM_LIMIT_BYTES),
        name="peer_ln2",
    )(h2t, x1, mod.reshape(bsz, 6, d), w_peer_q.T.astype(MXU_DTYPE),
      peer_keys.reshape(2 * PEER_HEADS, PEER_NKEYS, PEER_HALF).astype(MXU_DTYPE),
      peer_u.astype(MXU_DTYPE), peer_v.T.astype(MXU_DTYPE), ln_g.reshape(1, d), ln_b.reshape(1, d))


def kernel(x, c, positions, w_ada, b_ada, w_in, pool_w, pool_scale, q_norm_g, w_uq, kv_norm_g,
           w_ukv, w_out, ln1_g, ln1_b, w_peer_q, peer_keys, peer_u, peer_v, ln2_g, ln2_b):
    bsz, seq, d = x.shape
    assert w_ada.shape[0] == DEPTH
    tt = min(TOKEN_TILE, seq)
    tq = min(ATTN_Q_BLOCK, seq)
    cw = min(ATTN_CHUNK, tt)
    peer_tt = min(PEER_TOKEN_TILE, seq)
    for l in range(DEPTH):
        mod = _adaln(c, w_ada[l], b_ada[l])
        pool, qt, k, vt = _inproj(x, mod, positions, w_in[l], pool_w[l], pool_scale[l],
                                  q_norm_g[l], w_uq[l], kv_norm_g[l], w_ukv[l], tt, cw)
        att = _attention(qt, k, vt, tq, cw)
        x1, h2 = _outproj(x, pool, att, mod, w_out[l], ln1_g[l], ln1_b[l], tt)
        x = _peer(h2, x1.reshape(bsz * seq, d), mod, w_peer_q[l],
                  peer_keys[l], peer_u[l], peer_v[l], ln2_g[l], ln2_b[l], seq,
                  peer_tt).reshape(bsz, seq, d)
    return x
```

```python
import functools

import jax
import jax.numpy as jnp
from jax import lax
from jax.experimental import pallas as pl
from jax.experimental.pallas import tpu as pltpu

POOL_WINDOWS = (2, 4, 8, 16)
POOL_GROUP = 64
D_POOL = POOL_GROUP * len(POOL_WINDOWS)
V_HEAD = 128
QK_NOPE = 128
QK_ROPE = 64
QK_DIM = QK_NOPE + QK_ROPE
N_HEADS = 6
D_ATT = N_HEADS * V_HEAD
Q_LORA = 384
KV_LORA = 256
ROPE_THETA = 10000.0
PEER_HEADS = 8
PEER_NKEYS = 128
PEER_HALF = 128
PEER_TOPK = 16
DEPTH = 1
ALPHA = (2.0 * DEPTH) ** 0.25
LN_EPS = 1e-5
RMS_EPS = 1e-6
LOG2_E = 1.4426950408889634

LANES = 128
SUBLANES = 8
SLAB_ROWS = 16
VMEM_LIMIT_BYTES = 56 * 1024 * 1024

MXU_DTYPE = jnp.bfloat16
F32 = jnp.float32
NOT_SELECTED = 1.0e4

POOL_HALO = 16
TOKEN_TILE = 512
ATTN_Q_BLOCK = 1024
ATTN_CHUNK = 256
PEER_TOKEN_TILE = 512
PEER_ROWS_PER_CHUNK = 16
PEER_ROWS_PER_CHAIN = 4


def _dot(a, b):
    return jnp.dot(a, b, preferred_element_type=F32)


def _dot_nt(a, b):
    return lax.dot_general(a, b, (((1,), (1,)), ((), ())), preferred_element_type=F32)


def _layer_norm(y, g, b):
    mu = jnp.mean(y, axis=-1, keepdims=True)
    d = y - mu
    var = jnp.mean(d * d, axis=-1, keepdims=True)
    return d * lax.rsqrt(var + LN_EPS) * g + b


def _rms_norm(x, g):
    ms = jnp.mean(x * x, axis=-1, keepdims=True)
    return x * lax.rsqrt(ms + RMS_EPS) * g


def _adaln_kernel(c_ref, w_ref, b_ref, o_ref):
    c = c_ref[...]
    c_act = c * jax.nn.sigmoid(c)
    o_ref[...] = _dot(c_act.astype(MXU_DTYPE), w_ref[...].astype(MXU_DTYPE)) + b_ref[...]


def _adaln(c, w_ada, b_ada):
    bsz, d = c.shape
    n = w_ada.shape[1]
    blk = d
    return pl.pallas_call(
        _adaln_kernel,
        grid=(n // blk,),
        in_specs=[pl.BlockSpec((bsz, d), lambda i: (0, 0)),
                  pl.BlockSpec((d, blk), lambda i: (0, i)),
                  pl.BlockSpec((1, blk), lambda i: (0, i))],
        out_specs=pl.BlockSpec((bsz, blk), lambda i: (0, i)),
        out_shape=jax.ShapeDtypeStruct((bsz, n), F32),
        compiler_params=pltpu.CompilerParams(dimension_semantics=("arbitrary",),
                                             vmem_limit_bytes=VMEM_LIMIT_BYTES),
        name="adaln_mod",
    )(c, w_ada, b_ada.reshape(1, n))


def _inproj_kernel(x_ref, mod_ref, pos_ref, invf_ref, win_ref, wpool_ref, pscale_ref, qg_ref,
                   wuqt_ref, kvg_ref, wuk_ref, wuvt_ref, pool_ref, qt_ref, k_ref, vt_ref, ext_ref):
    j = pl.program_id(1)
    tt = x_ref.shape[1]
    x = x_ref[0]
    sh1 = mod_ref[0, 0:1, :]
    sc1 = mod_ref[0, 1:2, :]
    h = x * (1.0 + sc1) + sh1
    z = _dot(h.astype(MXU_DTYPE), win_ref[...])

    @pl.when(j == 0)
    def _():
        ext_ref[0:POOL_HALO, :] = jnp.zeros((POOL_HALO, D_POOL), F32)

    p = z[:, :D_POOL]
    ext_ref[POOL_HALO:POOL_HALO + tt, :] = p

    def shifted(d, col):
        return ext_ref[pl.ds(POOL_HALO - d, tt), col * LANES:(col + 1) * LANES]

    lane = lax.broadcasted_iota(jnp.int32, (tt, LANES), 1)
    low = lane < POOL_GROUP
    t_seq = (j * tt + lax.broadcasted_iota(jnp.int32, (tt, LANES), 0) + 1).astype(F32)
    mixed = []
    for col in range(2):
        w_lo, w_hi = POOL_WINDOWS[2 * col], POOL_WINDOWS[2 * col + 1]
        s_lo = shifted(0, col)
        for d in range(1, w_lo):
            s_lo = s_lo + shifted(d, col)
        s_hi = s_lo
        for d in range(w_lo, w_hi):
            s_hi = s_hi + shifted(d, col)
        win_sum = jnp.where(low, s_lo, s_hi)
        count = jnp.minimum(t_seq, jnp.where(low, float(w_lo), float(w_hi)))
        mixed.append(win_sum / count - p[:, col * LANES:(col + 1) * LANES])
    mixed = jnp.concatenate(mixed, axis=1)
    ext_ref[0:POOL_HALO, :] = ext_ref[tt:tt + POOL_HALO, :]
    pool = _dot(mixed.astype(MXU_DTYPE), wpool_ref[...]) * pscale_ref[...]
    pool_ref[0] = pool.astype(pool_ref.dtype)

    ang_t = invf_ref[...] * pos_ref[0].astype(F32)
    cos_t = jnp.cos(ang_t)
    sin_t = jnp.sin(ang_t)
    cos_t = jnp.concatenate([cos_t, cos_t], axis=0)
    sin_t = jnp.concatenate([sin_t, sin_t], axis=0)

    o1 = D_POOL
    o2 = o1 + Q_LORA
    o3 = o2 + KV_LORA
    cqn = _rms_norm(z[:, o1:o2], qg_ref[...]).astype(MXU_DTYPE)
    qq_t = _dot_nt(wuqt_ref[...], cqn)
    n_nope = N_HEADS * QK_NOPE
    n_rope = N_HEADS * QK_ROPE
    scale = QK_DIM ** -0.5 * LOG2_E
    cw = qt_ref.shape[4]
    for hd in range(N_HEADS):
        r0 = n_nope + hd * QK_ROPE
        rope_t = qq_t[r0:r0 + QK_ROPE] * cos_t + qq_t[r0 + n_rope:r0 + n_rope + QK_ROPE] * sin_t
        qh_t = (jnp.concatenate([qq_t[hd * QK_NOPE:(hd + 1) * QK_NOPE], rope_t], axis=0)
                * scale).astype(qt_ref.dtype)
        for cc in range(tt // cw):
            qt_ref[0, hd, cc] = qh_t[:, cc * cw:(cc + 1) * cw]

    ckvn = _rms_norm(z[:, o2:o3], kvg_ref[...]).astype(MXU_DTYPE)
    k_nope = _dot(ckvn, wuk_ref[...])
    v_t = _dot_nt(wuvt_ref[...], ckvn)
    cs = jnp.concatenate([cos_t, sin_t], axis=0).T
    kk = z[:, o3:] * cs
    k_rope = (kk + pltpu.roll(kk, QK_ROPE, 1))[:, :QK_ROPE]
    for hd in range(N_HEADS):
        kh = jnp.concatenate([k_nope[:, hd * QK_NOPE:(hd + 1) * QK_NOPE], k_rope], axis=1)
        k_ref[0, hd] = kh.astype(k_ref.dtype)
        vt_ref[0, hd, 0] = v_t[hd * V_HEAD:(hd + 1) * V_HEAD].astype(vt_ref.dtype)


def _rotate_half_cols(w):
    half = w.shape[-1] // 2
    return jnp.concatenate([-w[..., half:], w[..., :half]], axis=-1)


def _inproj(x, mod, positions, w_in, pool_w, pool_scale, q_norm_g, w_uq, kv_norm_g, w_ukv, tt, cw):
    bsz, seq, d = x.shape
    o3 = D_POOL + Q_LORA + KV_LORA
    w_in_x = jnp.concatenate([w_in, _rotate_half_cols(w_in[:, o3:])], axis=1).astype(MXU_DTYPE)
    n_groups = len(POOL_WINDOWS)
    w_pool_bd = jnp.zeros((D_POOL, D_POOL), F32)
    for g in range(n_groups):
        w_pool_bd = w_pool_bd.at[g * POOL_GROUP:(g + 1) * POOL_GROUP,
                                 g * POOL_GROUP:(g + 1) * POOL_GROUP].set(pool_w[g])
    w_pool_bd = w_pool_bd.astype(MXU_DTYPE)
    w_uq3 = w_uq.reshape(Q_LORA, N_HEADS, QK_DIM)
    w_q_rope = w_uq3[:, :, QK_NOPE:]
    w_uq_t = jnp.concatenate([w_uq3[:, :, :QK_NOPE].reshape(Q_LORA, N_HEADS * QK_NOPE),
                              w_q_rope.reshape(Q_LORA, N_HEADS * QK_ROPE),
                              _rotate_half_cols(w_q_rope).reshape(Q_LORA, N_HEADS * QK_ROPE)],
                             axis=1).T.astype(MXU_DTYPE)
    w_ukv3 = w_ukv.reshape(KV_LORA, N_HEADS, QK_NOPE + V_HEAD)
    w_uk = w_ukv3[:, :, :QK_NOPE].reshape(KV_LORA, N_HEADS * QK_NOPE).astype(MXU_DTYPE)
    w_uv_t = w_ukv3[:, :, QK_NOPE:].reshape(KV_LORA, N_HEADS * V_HEAD).T.astype(MXU_DTYPE)
    inv_freq = ROPE_THETA ** (-jnp.arange(0, QK_ROPE, 2, dtype=F32) / QK_ROPE)
    invf = inv_freq.reshape(QK_ROPE // 2, 1)
    full = lambda shape: pl.BlockSpec(shape, lambda b, j: (0,) * len(shape))
    return pl.pallas_call(
        _inproj_kernel,
        grid=(bsz, seq // tt),
        in_specs=[pl.BlockSpec((1, tt, d), lambda b, j: (b, j, 0)),
                  pl.BlockSpec((1, 6, d), lambda b, j: (b, 0, 0)),
                  pl.BlockSpec((1, 1, tt), lambda b, j: (b, 0, j)),
                  full(invf.shape),
                  full(w_in_x.shape),
                  full((D_POOL, D_POOL)),
                  full((1, D_POOL)),
                  full((1, Q_LORA)),
                  full(w_uq_t.shape),
                  full((1, KV_LORA)),
                  full(w_uk.shape),
                  full(w_uv_t.shape)],
        out_specs=[pl.BlockSpec((1, tt, D_POOL), lambda b, j: (b, j, 0)),
                   pl.BlockSpec((1, N_HEADS, tt // cw, QK_DIM, cw), lambda b, j: (b, 0, j, 0, 0)),
                   pl.BlockSpec((1, N_HEADS, tt, QK_DIM), lambda b, j: (b, 0, j, 0)),
                   pl.BlockSpec((1, N_HEADS, 1, V_HEAD, tt), lambda b, j: (b, 0, j, 0, 0))],
        out_shape=[jax.ShapeDtypeStruct((bsz, seq, D_POOL), MXU_DTYPE),
                   jax.ShapeDtypeStruct((bsz, N_HEADS, seq // cw, QK_DIM, cw), MXU_DTYPE),
                   jax.ShapeDtypeStruct((bsz, N_HEADS, seq, QK_DIM), MXU_DTYPE),
                   jax.ShapeDtypeStruct((bsz, N_HEADS, seq // tt, V_HEAD, tt), MXU_DTYPE)],
        scratch_shapes=[pltpu.VMEM((tt + POOL_HALO, D_POOL), F32)],
        compiler_params=pltpu.CompilerParams(dimension_semantics=("arbitrary", "arbitrary"),
                                             vmem_limit_bytes=VMEM_LIMIT_BYTES),
        name="in_proj",
    )(x, mod.reshape(bsz, 6, d), positions.reshape(bsz, 1, seq), invf, w_in_x, w_pool_bd,
      pool_scale.reshape(1, D_POOL), q_norm_g.reshape(1, Q_LORA), w_uq_t,
      kv_norm_g.reshape(1, KV_LORA), w_uk, w_uv_t)


def _attn_kernel(qt_ref, k_ref, vt_ref, o_ref, sa_ref, sb_ref, m_ref, l_ref, acc_ref, *, tq, tk, cw):
    seq = k_ref.shape[2]
    n_chunks = tq // cw
    assert tq == 2 * tk

    def q_block(qi, carry):
        q0 = pl.multiple_of(qi * tq, tq)
        m_ref[...] = jnp.full(m_ref.shape, -jnp.inf, F32)
        l_ref[...] = jnp.zeros(l_ref.shape, F32)
        acc_ref[...] = jnp.zeros(acc_ref.shape, F32)

        def active_chunks(rel):
            return [c for c in range(n_chunks) if rel is None or (c + 1) * cw > rel * tk]

        def scores_into(buf_ref, kj, rel):
            k0 = pl.multiple_of(kj * tk, tk)
            k = k_ref[0, 0, pl.ds(k0, tk), :]
            for c in active_chunks(rel):
                buf_ref[c] = _dot(k, qt_ref[0, 0, qi * n_chunks + c])

        def consume(buf_ref, kj, rel):
            vt = vt_ref[0, 0, kj]
            probs, alphas = {}, {}
            for c in active_chunks(rel):
                cols = slice(c * cw, (c + 1) * cw)
                s = buf_ref[c]
                if rel is not None and c * cw < (rel + 1) * tk:
                    kv_idx = rel * tk + lax.broadcasted_iota(jnp.int32, (tk, cw), 0)
                    q_idx = c * cw + lax.broadcasted_iota(jnp.int32, (tk, cw), 1)
                    s = jnp.where(q_idx >= kv_idx, s, -jnp.inf)
                m_prev = m_ref[:, cols]
                m_new = jnp.maximum(m_prev, jnp.max(s, axis=0, keepdims=True))
                alphas[c] = jnp.exp2(m_prev - m_new)
                p = jnp.exp2(s - m_new)
                l_ref[:, cols] = alphas[c] * l_ref[:, cols] + jnp.sum(p, axis=0, keepdims=True)
                m_ref[:, cols] = m_new
                probs[c] = p.astype(MXU_DTYPE)
            for c in active_chunks(rel):
                cols = slice(c * cw, (c + 1) * cw)
                acc_ref[:, cols] = alphas[c] * acc_ref[:, cols] + _dot(vt, probs[c])

        def pair(jj, c):
            scores_into(sb_ref, 2 * jj + 1, None)
            consume(sa_ref, 2 * jj, None)
            scores_into(sa_ref, 2 * jj + 2, None)
            consume(sb_ref, 2 * jj + 1, None)
            return c

        scores_into(sa_ref, 0, None)
        lax.fori_loop(0, qi, pair, 0)
        scores_into(sb_ref, 2 * qi + 1, 1)
        consume(sa_ref, 2 * qi, 0)
        consume(sb_ref, 2 * qi + 1, 1)
        out_t = acc_ref[...] / l_ref[...]
        o_ref[0, pl.ds(q0, tq), :] = out_t.T.astype(o_ref.dtype)
        return carry

    lax.fori_loop(0, seq // tq, q_block, 0)


def _attention(qt, k, vt, tq, cw):
    bsz, nh, seq, _ = k.shape
    tk = vt.shape[4]
    return pl.pallas_call(
        functools.partial(_attn_kernel, tq=tq, tk=tk, cw=cw),
        grid=(bsz, nh),
        in_specs=[pl.BlockSpec((1, 1, seq // cw, QK_DIM, cw), lambda b, h: (b, h, 0, 0, 0)),
                  pl.BlockSpec((1, 1, seq, QK_DIM), lambda b, h: (b, h, 0, 0)),
                  pl.BlockSpec((1, 1, seq // tk, V_HEAD, tk), lambda b, h: (b, h, 0, 0, 0))],
        out_specs=pl.BlockSpec((1, seq, V_HEAD), lambda b, h: (b, 0, h)),
        out_shape=jax.ShapeDtypeStruct((bsz, seq, nh * V_HEAD), MXU_DTYPE),
        scratch_shapes=[pltpu.VMEM((tq // cw, tk, cw), F32),
                        pltpu.VMEM((tq // cw, tk, cw), F32),
                        pltpu.VMEM((1, tq), F32),
                        pltpu.VMEM((1, tq), F32),
                        pltpu.VMEM((V_HEAD, tq), F32)],
        compiler_params=pltpu.CompilerParams(dimension_semantics=("arbitrary", "arbitrary"),
                                             vmem_limit_bytes=VMEM_LIMIT_BYTES),
        name="mla_attention",
    )(qt, k, vt)


def _outproj_kernel(x_ref, pool_ref, att_ref, mod_ref, wout_ref, g_ref, b_ref, x1_ref, h2t_ref):
    mix = _dot(pool_ref[0], wout_ref[0:D_POOL, :]) + _dot(att_ref[0], wout_ref[D_POOL:, :])
    g1 = mod_ref[0, 2:3, :]
    sh2 = mod_ref[0, 3:4, :]
    sc2 = mod_ref[0, 4:5, :]
    x1 = _layer_norm(ALPHA * x_ref[0] + g1 * mix, g_ref[...], b_ref[...])
    x1_ref[0] = x1
    h2t_ref[0] = (x1 * (1.0 + sc2) + sh2).T.astype(h2t_ref.dtype)


def _outproj(x, pool, att, mod, w_out, ln_g, ln_b, tt):
    bsz, seq, d = x.shape
    tile = lambda n: pl.BlockSpec((1, tt, n), lambda b, j: (b, j, 0))
    full = lambda shape: pl.BlockSpec(shape, lambda b, j: (0,) * len(shape))
    return pl.pallas_call(
        _outproj_kernel,
        grid=(bsz, seq // tt),
        in_specs=[tile(d), tile(D_POOL), tile(D_ATT),
                  pl.BlockSpec((1, 6, d), lambda b, j: (b, 0, 0)),
                  full(w_out.shape), full((1, d)), full((1, d))],
        out_specs=[tile(d), pl.BlockSpec((1, d, tt), lambda b, j: (b, 0, j))],
        out_shape=[jax.ShapeDtypeStruct((bsz, seq, d), F32),
                   jax.ShapeDtypeStruct((bsz, d, seq), MXU_DTYPE)],
        compiler_params=pltpu.CompilerParams(dimension_semantics=("arbitrary", "arbitrary"),
                                             vmem_limit_bytes=VMEM_LIMIT_BYTES),
        name="out_proj_ln1",
    )(x, pool, att, mod.reshape(bsz, 6, d), w_out.astype(MXU_DTYPE), ln_g.reshape(1, d),
      ln_b.reshape(1, d))


def _extract_topk(s, k):
    n = s.shape[0]
    iota = lax.broadcasted_iota(jnp.int32, s.shape, 0).astype(F32)
    rank = jnp.full(s.shape, NOT_SELECTED, F32)
    vals = []
    sub = lax.broadcasted_iota(jnp.int32, (SUBLANES, s.shape[1]), 0).astype(F32)
    for r in range(k):
        nodes = [(s[g:g + SUBLANES], float(g)) for g in range(0, n, SUBLANES)]
        while len(nodes) > 1:
            merged = []
            for a in range(0, len(nodes) - 1, 2):
                (va, ga), (vb, gb) = nodes[a], nodes[a + 1]
                take_b = vb > va
                merged.append((jnp.where(take_b, vb, va), jnp.where(take_b, gb, ga)))
            if len(nodes) % 2:
                merged.append(nodes[-1])
            nodes = merged
        v8, g8 = nodes[0]
        m = jnp.max(v8, axis=0, keepdims=True)
        first = jnp.min(jnp.where(v8 == m, g8 + sub, float(n)), axis=0, keepdims=True)
        sel = iota == first
        rank = jnp.where(sel, float(r + 1), rank)
        s = jnp.where(sel, -jnp.inf, s)
        vals.append(m)
    return rank, jnp.concatenate(vals, axis=0)


def _peer_kernel(h2t_ref, x1_ref, mod_ref, wq_ref, keys_ref, u_ref, vt_ref, g_ref, b_ref, o_ref,
                 qt_ref, lim1_ref, e1_ref, r2_ref, e2_ref, gate_ref, a0_ref, a1_ref, w0_ref, w1_ref,
                 acc_ref):
    a_refs = (a0_ref, a1_ref)
    w_refs = (w0_ref, w1_ref)
    c = pl.program_id(1)
    tt = h2t_ref.shape[2]
    k = PEER_TOPK
    slabs = PEER_NKEYS // SLAB_ROWS
    tiles = PEER_NKEYS // SUBLANES
    n_tb = tt // LANES

    @pl.when(c == 0)
    def _():
        acc_ref[...] = jnp.zeros(acc_ref.shape, F32)
        qt_ref[...] = _dot(wq_ref[...], h2t_ref[0]).astype(qt_ref.dtype)

        def head(hd, carry):
            r0 = pl.multiple_of(hd * 2 * PEER_HALF, 2 * PEER_HALF)
            s1 = _dot(keys_ref[2 * hd], qt_ref[pl.ds(r0, PEER_HALF), :])
            s2 = _dot(keys_ref[2 * hd + 1], qt_ref[pl.ds(r0 + PEER_HALF, PEER_HALF), :])
            rank1, a = _extract_topk(s1, k)
            rank2, b = _extract_topk(s2, k)
            blocks = [a[0:1] + b, a[1:2] + b[0:8]]
            blocks += [a[r:r + 1] + b[0:8] for r in range(2, 8)]
            blocks += [a[8:16] + b[0:1]]
            cand = jnp.concatenate(blocks, axis=0)
            crank, _ = _extract_topk(cand, k)
            sel = crank < NOT_SELECTED
            sel_f = jnp.where(sel, 1.0, 0.0)
            row_len = [jnp.sum(sel_f[0:16], axis=0, keepdims=True),
                       jnp.sum(sel_f[16:24], axis=0, keepdims=True)]
            row_len += [jnp.sum(sel_f[24 + 8 * r:32 + 8 * r], axis=0, keepdims=True) for r in range(6)]
            row_len += [sel_f[72 + r:73 + r] for r in range(8)]
            top = a[0:1] + b[0:1]
            z = jnp.sum(jnp.where(sel, jnp.exp(cand - top), 0.0), axis=0, keepdims=True)
            lim1 = jnp.zeros(s1.shape, F32)
            for r in range(k):
                lim1 = jnp.where(rank1 == float(r + 1), row_len[r], lim1)
            e1 = jnp.exp(s1 - a[0:1]) / z
            r2 = rank2
            e2 = jnp.exp(s2 - b[0:1])
            for tb in range(n_tb):
                ls = slice(tb * LANES, (tb + 1) * LANES)
                lim1_ref[hd, tb] = lim1[:, ls].reshape(tiles, SUBLANES, LANES)
                e1_ref[hd, tb] = e1[:, ls].reshape(tiles, SUBLANES, LANES)
                r2_ref[hd, tb] = r2[:, ls].reshape(slabs, SLAB_ROWS, LANES)
                e2_ref[hd, tb] = e2[:, ls].reshape(slabs, SLAB_ROWS, LANES)
            return carry

        lax.fori_loop(0, PEER_HEADS, head, 0)

    rows = u_ref.shape[0] // PEER_NKEYS
    row_tiles = rows // SUBLANES
    half = SUBLANES // 2

    def gate_block(idx, carry):
        tile = idx // n_tb
        tb = idx % n_tb
        i_tile = c * row_tiles + tile
        for hf in range(2):
            accs = [[None] * slabs for _ in range(half)]
            for hd in range(PEER_HEADS):
                lim_t = lim1_ref[hd, tb, i_tile]
                e1_t = e1_ref[hd, tb, i_tile]
                lims = [jnp.broadcast_to(lim_t[hf * half + ii:hf * half + ii + 1], (SLAB_ROWS, LANES))
                        for ii in range(half)]
                e1rs = [jnp.broadcast_to(e1_t[hf * half + ii:hf * half + ii + 1], (SLAB_ROWS, LANES))
                        for ii in range(half)]
                for sl in range(slabs):
                    r2 = r2_ref[hd, tb, sl]
                    e2 = e2_ref[hd, tb, sl]
                    for ii in range(half):
                        term = jnp.where(r2 <= lims[ii], e2 * e1rs[ii], 0.0)
                        accs[ii][sl] = term if accs[ii][sl] is None else accs[ii][sl] + term
            for ii in range(half):
                for sl in range(slabs):
                    gate_ref[tb, (tile * SUBLANES + hf * half + ii) * slabs + sl] = accs[ii][sl]
        return carry

    lax.fori_loop(0, row_tiles * n_tb, gate_block, 0)

    h2t = h2t_ref[0]
    chain_rows = PEER_ROWS_PER_CHAIN
    chain_e = chain_rows * PEER_NKEYS
    n_chains = rows // chain_rows

    def expert_pre(ch):
        a_refs[ch % 2][...] = _dot(u_ref[ch * chain_e:(ch + 1) * chain_e, :], h2t)

    def gated(ch):
        for tb in range(n_tb):
            ls = slice(tb * LANES, (tb + 1) * LANES)
            for sb in range(chain_rows * slabs):
                r0 = sb * SLAB_ROWS
                a_blk = a_refs[ch % 2][r0:r0 + SLAB_ROWS, ls]
                gelu = 0.5 * a_blk * (1.0 + lax.erf(a_blk * (2.0 ** -0.5)))
                w_blk = gate_ref[tb, ch * chain_rows * slabs + sb] * gelu
                w_refs[ch % 2][r0:r0 + SLAB_ROWS, ls] = w_blk.astype(MXU_DTYPE)

    def expert_post(ch):
        acc_ref[...] += _dot(vt_ref[:, ch * chain_e:(ch + 1) * chain_e], w_refs[ch % 2][...])

    expert_pre(0)
    for ch in range(n_chains):
        if ch > 0:
            expert_post(ch - 1)
        if ch + 1 < n_chains:
            expert_pre(ch + 1)
        gated(ch)
    expert_post(n_chains - 1)

    @pl.when(c == pl.num_programs(1) - 1)
    def _():
        ffn = acc_ref[...].T
        g2 = mod_ref[0, 5:6, :]
        o_ref[...] = _layer_norm(ALPHA * x1_ref[...] + g2 * ffn, g_ref[...], b_ref[...])


def _peer(h2t, x1, mod, w_peer_q, peer_keys, peer_u, peer_v, ln_g, ln_b, seq, tt):
    n_tok, d = x1.shape
    bsz = n_tok // seq
    rows = PEER_ROWS_PER_CHUNK
    ec = rows * PEER_NKEYS
    n_exp = peer_u.shape[0]
    n_q = w_peer_q.shape[1]
    tiles_per_seq = seq // tt
    full = lambda shape: pl.BlockSpec(shape, lambda t, c: (0,) * len(shape))
    assert rows % SUBLANES == 0 and tt % LANES == 0
    n_tb = tt // LANES
    hk = (PEER_HEADS, n_tb, PEER_NKEYS // SUBLANES, SUBLANES, LANES)
    hk_slab = (PEER_HEADS, n_tb, PEER_NKEYS // SLAB_ROWS, SLAB_ROWS, LANES)
    gate_shape = (n_tb, rows * PEER_NKEYS // SLAB_ROWS, SLAB_ROWS, LANES)
    return pl.pallas_call(
        _peer_kernel,
        grid=(n_tok // tt, n_exp // ec),
        in_specs=[pl.BlockSpec((1, d, tt), lambda t, c: (t // tiles_per_seq, 0, t % tiles_per_seq)),
                  pl.BlockSpec((tt, d), lambda t, c: (t, 0)),
                  pl.BlockSpec((1, 6, d), lambda t, c: (t // tiles_per_seq, 0, 0)),
                  full((n_q, d)),
                  full((2 * PEER_HEADS, PEER_NKEYS, PEER_HALF)),
                  pl.BlockSpec((ec, d), lambda t, c: (c, 0)),
                  pl.BlockSpec((d, ec), lambda t, c: (0, c)),
                  full((1, d)), full((1, d))],
        out_specs=pl.BlockSpec((tt, d), lambda t, c: (t, 0)),
        out_shape=jax.ShapeDtypeStruct((n_tok, d), F32),
        scratch_shapes=[pltpu.VMEM((n_q, tt), MXU_DTYPE),
                        pltpu.VMEM(hk, F32), pltpu.VMEM(hk, F32),
                        pltpu.VMEM(hk_slab, F32), pltpu.VMEM(hk_slab, F32),
                        pltpu.VMEM(gate_shape, F32),
                        pltpu.VMEM((PEER_ROWS_PER_CHAIN * PEER_NKEYS, tt), F32),
                        pltpu.VMEM((PEER_ROWS_PER_CHAIN * PEER_NKEYS, tt), F32),
                        pltpu.VMEM((PEER_ROWS_PER_CHAIN * PEER_NKEYS, tt), MXU_DTYPE),
                        pltpu.VMEM((PEER_ROWS_PER_CHAIN * PEER_NKEYS, tt), MXU_DTYPE),
                        pltpu.VMEM((d, tt), F32)],
        compiler_params=pltpu.CompilerParams(dimension_semantics=("arbitrary", "arbitrary"),
                                             vmem_limit_bytes=VMEM_LIMIT_BYTES),
        name="peer_ln2",
    )(h2t, x1, mod.reshape(bsz, 6, d), w_peer_q.T.astype(MXU_DTYPE),
      peer_keys.reshape(2 * PEER_HEADS, PEER_NKEYS, PEER_HALF).astype(MXU_DTYPE),
      peer_u.astype(MXU_DTYPE), peer_v.T.astype(MXU_DTYPE), ln_g.reshape(1, d), ln_b.reshape(1, d))


def kernel(x, c, positions, w_ada, b_ada, w_in, pool_w, pool_scale, q_norm_g, w_uq, kv_norm_g,
           w_ukv, w_out, ln1_g, ln1_b, w_peer_q, peer_keys, peer_u, peer_v, ln2_g, ln2_b):
    bsz, seq, d = x.shape
    assert w_ada.shape[0] == DEPTH
    tt = min(TOKEN_TILE, seq)
    tq = min(ATTN_Q_BLOCK, seq)
    cw = min(ATTN_CHUNK, tt)
    peer_tt = min(PEER_TOKEN_TILE, seq)
    for l in range(DEPTH):
        mod = _adaln(c, w_ada[l], b_ada[l])
        pool, qt, k, vt = _inproj(x, mod, positions, w_in[l], pool_w[l], pool_scale[l],
                                  q_norm_g[l], w_uq[l], kv_norm_g[l], w_ukv[l], tt, cw)
        att = _attention(qt, k, vt, tq, cw)
        x1, h2 = _outproj(x, pool, att, mod, w_out[l], ln1_g[l], ln1_b[l], tt)
        x = _peer(h2, x1.reshape(bsz * seq, d), mod, w_peer_q[l],
                  peer_keys[l], peer_u[l], peer_v[l], ln2_g[l], ln2_b[l], seq,
                  peer_tt).reshape(bsz, seq, d)
    return x
```

```python
import functools

import jax
import jax.numpy as jnp
from jax import lax
from jax.experimental import pallas as pl
from jax.experimental.pallas import tpu as pltpu

POOL_WINDOWS = (2, 4, 8, 16)
POOL_GROUP = 64
D_POOL = POOL_GROUP * len(POOL_WINDOWS)
V_HEAD = 128
QK_NOPE = 128
QK_ROPE = 64
QK_DIM = QK_NOPE + QK_ROPE
N_HEADS = 6
D_ATT = N_HEADS * V_HEAD
Q_LORA = 384
KV_LORA = 256
ROPE_THETA = 10000.0
PEER_HEADS = 8
PEER_NKEYS = 128
PEER_HALF = 128
PEER_TOPK = 16
DEPTH = 1
ALPHA = (2.0 * DEPTH) ** 0.25
LN_EPS = 1e-5
RMS_EPS = 1e-6
LOG2_E = 1.4426950408889634

LANES = 128
SUBLANES = 8
SLAB_ROWS = 16
VMEM_LIMIT_BYTES = 60000 * 1024

MXU_DTYPE = jnp.bfloat16
F32 = jnp.float32
NOT_SELECTED = 1.0e4

POOL_HALO = 16
TOKEN_TILE = 512
ATTN_Q_BLOCK = 1024
ATTN_CHUNK = 256
PEER_TOKEN_TILE = 512
PEER_ROWS_PER_CHUNK = 16
PEER_ROWS_PER_CHAIN = 8


def _dot(a, b):
    return jnp.dot(a, b, preferred_element_type=F32)


def _dot_nt(a, b):
    return lax.dot_general(a, b, (((1,), (1,)), ((), ())), preferred_element_type=F32)


def _layer_norm(y, g, b):
    mu = jnp.mean(y, axis=-1, keepdims=True)
    d = y - mu
    var = jnp.mean(d * d, axis=-1, keepdims=True)
    return d * lax.rsqrt(var + LN_EPS) * g + b


def _rms_norm(x, g):
    ms = jnp.mean(x * x, axis=-1, keepdims=True)
    return x * lax.rsqrt(ms + RMS_EPS) * g


def _adaln_kernel(c_ref, w_ref, b_ref, o_ref):
    c = c_ref[...]
    c_act = c * jax.nn.sigmoid(c)
    o_ref[...] = _dot(c_act.astype(MXU_DTYPE), w_ref[...].astype(MXU_DTYPE)) + b_ref[...]


def _adaln(c, w_ada, b_ada):
    bsz, d = c.shape
    n = w_ada.shape[1]
    blk = d
    return pl.pallas_call(
        _adaln_kernel,
        grid=(n // blk,),
        in_specs=[pl.BlockSpec((bsz, d), lambda i: (0, 0)),
                  pl.BlockSpec((d, blk), lambda i: (0, i)),
                  pl.BlockSpec((1, blk), lambda i: (0, i))],
        out_specs=pl.BlockSpec((bsz, blk), lambda i: (0, i)),
        out_shape=jax.ShapeDtypeStruct((bsz, n), F32),
        compiler_params=pltpu.CompilerParams(dimension_semantics=("arbitrary",),
                                             vmem_limit_bytes=VMEM_LIMIT_BYTES),
        name="adaln_mod",
    )(c, w_ada, b_ada.reshape(1, n))


def _inproj_kernel(x_ref, mod_ref, pos_ref, invf_ref, win_ref, wpool_ref, pscale_ref, qg_ref,
                   wuqt_ref, kvg_ref, wuk_ref, wuvt_ref, pool_ref, qt_ref, k_ref, vt_ref, ext_ref):
    j = pl.program_id(1)
    tt = x_ref.shape[1]
    x = x_ref[0]
    sh1 = mod_ref[0, 0:1, :]
    sc1 = mod_ref[0, 1:2, :]
    h = x * (1.0 + sc1) + sh1
    z = _dot(h.astype(MXU_DTYPE), win_ref[...])

    @pl.when(j == 0)
    def _():
        ext_ref[0:POOL_HALO, :] = jnp.zeros((POOL_HALO, D_POOL), F32)

    p = z[:, :D_POOL]
    ext_ref[POOL_HALO:POOL_HALO + tt, :] = p

    def shifted(d, col):
        return ext_ref[pl.ds(POOL_HALO - d, tt), col * LANES:(col + 1) * LANES]

    lane = lax.broadcasted_iota(jnp.int32, (tt, LANES), 1)
    low = lane < POOL_GROUP
    t_seq = (j * tt + lax.broadcasted_iota(jnp.int32, (tt, LANES), 0) + 1).astype(F32)
    mixed = []
    for col in range(2):
        w_lo, w_hi = POOL_WINDOWS[2 * col], POOL_WINDOWS[2 * col + 1]
        s_lo = shifted(0, col)
        for d in range(1, w_lo):
            s_lo = s_lo + shifted(d, col)
        s_hi = s_lo
        for d in range(w_lo, w_hi):
            s_hi = s_hi + shifted(d, col)
        win_sum = jnp.where(low, s_lo, s_hi)
        count = jnp.minimum(t_seq, jnp.where(low, float(w_lo), float(w_hi)))
        mixed.append(win_sum / count - p[:, col * LANES:(col + 1) * LANES])
    mixed = jnp.concatenate(mixed, axis=1)
    ext_ref[0:POOL_HALO, :] = ext_ref[tt:tt + POOL_HALO, :]
    pool = _dot(mixed.astype(MXU_DTYPE), wpool_ref[...]) * pscale_ref[...]
    pool_ref[0] = pool.astype(pool_ref.dtype)

    ang_t = invf_ref[...] * pos_ref[0].astype(F32)
    cos_t = jnp.cos(ang_t)
    sin_t = jnp.sin(ang_t)
    cos_t = jnp.concatenate([cos_t, cos_t], axis=0)
    sin_t = jnp.concatenate([sin_t, sin_t], axis=0)

    o1 = D_POOL
    o2 = o1 + Q_LORA
    o3 = o2 + KV_LORA
    cqn = _rms_norm(z[:, o1:o2], qg_ref[...]).astype(MXU_DTYPE)
    qq_t = _dot_nt(wuqt_ref[...], cqn)
    n_nope = N_HEADS * QK_NOPE
    n_rope = N_HEADS * QK_ROPE
    scale = QK_DIM ** -0.5 * LOG2_E
    cw = qt_ref.shape[4]
    for hd in range(N_HEADS):
        r0 = n_nope + hd * QK_ROPE
        rope_t = qq_t[r0:r0 + QK_ROPE] * cos_t + qq_t[r0 + n_rope:r0 + n_rope + QK_ROPE] * sin_t
        qh_t = (jnp.concatenate([qq_t[hd * QK_NOPE:(hd + 1) * QK_NOPE], rope_t], axis=0)
                * scale).astype(qt_ref.dtype)
        for cc in range(tt // cw):
            qt_ref[0, hd, cc] = qh_t[:, cc * cw:(cc + 1) * cw]

    ckvn = _rms_norm(z[:, o2:o3], kvg_ref[...]).astype(MXU_DTYPE)
    k_nope = _dot(ckvn, wuk_ref[...])
    v_t = _dot_nt(wuvt_ref[...], ckvn)
    cs = jnp.concatenate([cos_t, sin_t], axis=0).T
    kk = z[:, o3:] * cs
    k_rope = (kk + pltpu.roll(kk, QK_ROPE, 1))[:, :QK_ROPE]
    for hd in range(N_HEADS):
        kh = jnp.concatenate([k_nope[:, hd * QK_NOPE:(hd + 1) * QK_NOPE], k_rope], axis=1)
        k_ref[0, hd] = kh.astype(k_ref.dtype)
        vt_ref[0, hd, 0] = v_t[hd * V_HEAD:(hd + 1) * V_HEAD].astype(vt_ref.dtype)


def _rotate_half_cols(w):
    half = w.shape[-1] // 2
    return jnp.concatenate([-w[..., half:], w[..., :half]], axis=-1)


def _inproj(x, mod, positions, w_in, pool_w, pool_scale, q_norm_g, w_uq, kv_norm_g, w_ukv, tt, cw):
    bsz, seq, d = x.shape
    o3 = D_POOL + Q_LORA + KV_LORA
    w_in_x = jnp.concatenate([w_in, _rotate_half_cols(w_in[:, o3:])], axis=1).astype(MXU_DTYPE)
    n_groups = len(POOL_WINDOWS)
    w_pool_bd = jnp.zeros((D_POOL, D_POOL), F32)
    for g in range(n_groups):
        w_pool_bd = w_pool_bd.at[g * POOL_GROUP:(g + 1) * POOL_GROUP,
                                 g * POOL_GROUP:(g + 1) * POOL_GROUP].set(pool_w[g])
    w_pool_bd = w_pool_bd.astype(MXU_DTYPE)
    w_uq3 = w_uq.reshape(Q_LORA, N_HEADS, QK_DIM)
    w_q_rope = w_uq3[:, :, QK_NOPE:]
    w_uq_t = jnp.concatenate([w_uq3[:, :, :QK_NOPE].reshape(Q_LORA, N_HEADS * QK_NOPE),
                              w_q_rope.reshape(Q_LORA, N_HEADS * QK_ROPE),
                              _rotate_half_cols(w_q_rope).reshape(Q_LORA, N_HEADS * QK_ROPE)],
                             axis=1).T.astype(MXU_DTYPE)
    w_ukv3 = w_ukv.reshape(KV_LORA, N_HEADS, QK_NOPE + V_HEAD)
    w_uk = w_ukv3[:, :, :QK_NOPE].reshape(KV_LORA, N_HEADS * QK_NOPE).astype(MXU_DTYPE)
    w_uv_t = w_ukv3[:, :, QK_NOPE:].reshape(KV_LORA, N_HEADS * V_HEAD).T.astype(MXU_DTYPE)
    inv_freq = ROPE_THETA ** (-jnp.arange(0, QK_ROPE, 2, dtype=F32) / QK_ROPE)
    invf = inv_freq.reshape(QK_ROPE // 2, 1)
    full = lambda shape: pl.BlockSpec(shape, lambda b, j: (0,) * len(shape))
    return pl.pallas_call(
        _inproj_kernel,
        grid=(bsz, seq // tt),
        in_specs=[pl.BlockSpec((1, tt, d), lambda b, j: (b, j, 0)),
                  pl.BlockSpec((1, 6, d), lambda b, j: (b, 0, 0)),
                  pl.BlockSpec((1, 1, tt), lambda b, j: (b, 0, j)),
                  full(invf.shape),
                  full(w_in_x.shape),
                  full((D_POOL, D_POOL)),
                  full((1, D_POOL)),
                  full((1, Q_LORA)),
                  full(w_uq_t.shape),
                  full((1, KV_LORA)),
                  full(w_uk.shape),
                  full(w_uv_t.shape)],
        out_specs=[pl.BlockSpec((1, tt, D_POOL), lambda b, j: (b, j, 0)),
                   pl.BlockSpec((1, N_HEADS, tt // cw, QK_DIM, cw), lambda b, j: (b, 0, j, 0, 0)),
                   pl.BlockSpec((1, N_HEADS, tt, QK_DIM), lambda b, j: (b, 0, j, 0)),
                   pl.BlockSpec((1, N_HEADS, 1, V_HEAD, tt), lambda b, j: (b, 0, j, 0, 0))],
        out_shape=[jax.ShapeDtypeStruct((bsz, seq, D_POOL), MXU_DTYPE),
                   jax.ShapeDtypeStruct((bsz, N_HEADS, seq // cw, QK_DIM, cw), MXU_DTYPE),
                   jax.ShapeDtypeStruct((bsz, N_HEADS, seq, QK_DIM), MXU_DTYPE),
                   jax.ShapeDtypeStruct((bsz, N_HEADS, seq // tt, V_HEAD, tt), MXU_DTYPE)],
        scratch_shapes=[pltpu.VMEM((tt + POOL_HALO, D_POOL), F32)],
        compiler_params=pltpu.CompilerParams(dimension_semantics=("arbitrary", "arbitrary"),
                                             vmem_limit_bytes=VMEM_LIMIT_BYTES),
        name="in_proj",
    )(x, mod.reshape(bsz, 6, d), positions.reshape(bsz, 1, seq), invf, w_in_x, w_pool_bd,
      pool_scale.reshape(1, D_POOL), q_norm_g.reshape(1, Q_LORA), w_uq_t,
      kv_norm_g.reshape(1, KV_LORA), w_uk, w_uv_t)


def _attn_kernel(qt_ref, k_ref, vt_ref, o_ref, sa_ref, sb_ref, m_ref, l_ref, acc_ref, *, tq, tk, cw):
    seq = k_ref.shape[2]
    n_chunks = tq // cw
    assert tq == 2 * tk

    def q_block(qi, carry):
        q0 = pl.multiple_of(qi * tq, tq)
        m_ref[...] = jnp.full(m_ref.shape, -jnp.inf, F32)
        l_ref[...] = jnp.zeros(l_ref.shape, F32)
        acc_ref[...] = jnp.zeros(acc_ref.shape, F32)

        def active_chunks(rel):
            return [c for c in range(n_chunks) if rel is None or (c + 1) * cw > rel * tk]

        def scores_into(buf_ref, kj, rel):
            k0 = pl.multiple_of(kj * tk, tk)
            k = k_ref[0, 0, pl.ds(k0, tk), :]
            for c in active_chunks(rel):
                buf_ref[c] = _dot(k, qt_ref[0, 0, qi * n_chunks + c])

        def consume(buf_ref, kj, rel):
            vt = vt_ref[0, 0, kj]
            probs, alphas = {}, {}
            for c in active_chunks(rel):
                cols = slice(c * cw, (c + 1) * cw)
                s = buf_ref[c]
                if rel is not None and c * cw < (rel + 1) * tk:
                    kv_idx = rel * tk + lax.broadcasted_iota(jnp.int32, (tk, cw), 0)
                    q_idx = c * cw + lax.broadcasted_iota(jnp.int32, (tk, cw), 1)
                    s = jnp.where(q_idx >= kv_idx, s, -jnp.inf)
                m_prev = m_ref[:, cols]
                m_new = jnp.maximum(m_prev, jnp.max(s, axis=0, keepdims=True))
                alphas[c] = jnp.exp2(m_prev - m_new)
                p = jnp.exp2(s - m_new)
                l_ref[:, cols] = alphas[c] * l_ref[:, cols] + jnp.sum(p, axis=0, keepdims=True)
                m_ref[:, cols] = m_new
                probs[c] = p.astype(MXU_DTYPE)
            for c in active_chunks(rel):
                cols = slice(c * cw, (c + 1) * cw)
                acc_ref[:, cols] = alphas[c] * acc_ref[:, cols] + _dot(vt, probs[c])

        def pair(jj, c):
            scores_into(sb_ref, 2 * jj + 1, None)
            consume(sa_ref, 2 * jj, None)
            scores_into(sa_ref, 2 * jj + 2, None)
            consume(sb_ref, 2 * jj + 1, None)
            return c

        scores_into(sa_ref, 0, None)
        lax.fori_loop(0, qi, pair, 0)
        scores_into(sb_ref, 2 * qi + 1, 1)
        consume(sa_ref, 2 * qi, 0)
        consume(sb_ref, 2 * qi + 1, 1)
        out_t = acc_ref[...] / l_ref[...]
        o_ref[0, pl.ds(q0, tq), :] = out_t.T.astype(o_ref.dtype)
        return carry

    lax.fori_loop(0, seq // tq, q_block, 0)


def _attention(qt, k, vt, tq, cw):
    bsz, nh, seq, _ = k.shape
    tk = vt.shape[4]
    return pl.pallas_call(
        functools.partial(_attn_kernel, tq=tq, tk=tk, cw=cw),
        grid=(bsz, nh),
        in_specs=[pl.BlockSpec((1, 1, seq // cw, QK_DIM, cw), lambda b, h: (b, h, 0, 0, 0)),
                  pl.BlockSpec((1, 1, seq, QK_DIM), lambda b, h: (b, h, 0, 0)),
                  pl.BlockSpec((1, 1, seq // tk, V_HEAD, tk), lambda b, h: (b, h, 0, 0, 0))],
        out_specs=pl.BlockSpec((1, seq, V_HEAD), lambda b, h: (b, 0, h)),
        out_shape=jax.ShapeDtypeStruct((bsz, seq, nh * V_HEAD), MXU_DTYPE),
        scratch_shapes=[pltpu.VMEM((tq // cw, tk, cw), F32),
                        pltpu.VMEM((tq // cw, tk, cw), F32),
                        pltpu.VMEM((1, tq), F32),
                        pltpu.VMEM((1, tq), F32),
                        pltpu.VMEM((V_HEAD, tq), F32)],
        compiler_params=pltpu.CompilerParams(dimension_semantics=("arbitrary", "arbitrary"),
                                             vmem_limit_bytes=VMEM_LIMIT_BYTES),
        name="mla_attention",
    )(qt, k, vt)


def _outproj_kernel(x_ref, pool_ref, att_ref, mod_ref, wout_ref, g_ref, b_ref, x1_ref, h2t_ref):
    mix = _dot(pool_ref[0], wout_ref[0:D_POOL, :]) + _dot(att_ref[0], wout_ref[D_POOL:, :])
    g1 = mod_ref[0, 2:3, :]
    sh2 = mod_ref[0, 3:4, :]
    sc2 = mod_ref[0, 4:5, :]
    x1 = _layer_norm(ALPHA * x_ref[0] + g1 * mix, g_ref[...], b_ref[...])
    x1_ref[0] = x1
    h2t_ref[0] = (x1 * (1.0 + sc2) + sh2).T.astype(h2t_ref.dtype)


def _outproj(x, pool, att, mod, w_out, ln_g, ln_b, tt):
    bsz, seq, d = x.shape
    tile = lambda n: pl.BlockSpec((1, tt, n), lambda b, j: (b, j, 0))
    full = lambda shape: pl.BlockSpec(shape, lambda b, j: (0,) * len(shape))
    return pl.pallas_call(
        _outproj_kernel,
        grid=(bsz, seq // tt),
        in_specs=[tile(d), tile(D_POOL), tile(D_ATT),
                  pl.BlockSpec((1, 6, d), lambda b, j: (b, 0, 0)),
                  full(w_out.shape), full((1, d)), full((1, d))],
        out_specs=[tile(d), pl.BlockSpec((1, d, tt), lambda b, j: (b, 0, j))],
        out_shape=[jax.ShapeDtypeStruct((bsz, seq, d), F32),
                   jax.ShapeDtypeStruct((bsz, d, seq), MXU_DTYPE)],
        compiler_params=pltpu.CompilerParams(dimension_semantics=("arbitrary", "arbitrary"),
                                             vmem_limit_bytes=VMEM_LIMIT_BYTES),
        name="out_proj_ln1",
    )(x, pool, att, mod.reshape(bsz, 6, d), w_out.astype(MXU_DTYPE), ln_g.reshape(1, d),
      ln_b.reshape(1, d))


def _extract_topk(s, k):
    n = s.shape[0]
    iota = lax.broadcasted_iota(jnp.int32, s.shape, 0).astype(F32)
    rank = jnp.full(s.shape, NOT_SELECTED, F32)
    vals = []
    sub = lax.broadcasted_iota(jnp.int32, (SUBLANES, s.shape[1]), 0).astype(F32)
    for r in range(k):
        nodes = [(s[g:g + SUBLANES], float(g)) for g in range(0, n, SUBLANES)]
        while len(nodes) > 1:
            merged = []
            for a in range(0, len(nodes) - 1, 2):
                (va, ga), (vb, gb) = nodes[a], nodes[a + 1]
                take_b = vb > va
                merged.append((jnp.where(take_b, vb, va), jnp.where(take_b, gb, ga)))
            if len(nodes) % 2:
                merged.append(nodes[-1])
            nodes = merged
        v8, g8 = nodes[0]
        m = jnp.max(v8, axis=0, keepdims=True)
        first = jnp.min(jnp.where(v8 == m, g8 + sub, float(n)), axis=0, keepdims=True)
        sel = iota == first
        rank = jnp.where(sel, float(r + 1), rank)
        s = jnp.where(sel, -jnp.inf, s)
        vals.append(m)
    return rank, jnp.concatenate(vals, axis=0)


def _peer_kernel(h2t_ref, x1_ref, mod_ref, wq_ref, keys_ref, u_ref, vt_ref, g_ref, b_ref, o_ref,
                 qt_ref, lim1_ref, e1_ref, r2_ref, e2_ref, gate_ref, a0_ref, a1_ref, w0_ref, w1_ref,
                 acc_ref):
    a_refs = (a0_ref, a1_ref)
    w_refs = (w0_ref, w1_ref)
    c = pl.program_id(1)
    tt = h2t_ref.shape[2]
    k = PEER_TOPK
    slabs = PEER_NKEYS // SLAB_ROWS
    tiles = PEER_NKEYS // SUBLANES
    n_tb = tt // LANES

    @pl.when(c == 0)
    def _():
        acc_ref[...] = jnp.zeros(acc_ref.shape, F32)
        qt_ref[...] = _dot(wq_ref[...], h2t_ref[0]).astype(qt_ref.dtype)

        def head(hd, carry):
            r0 = pl.multiple_of(hd * 2 * PEER_HALF, 2 * PEER_HALF)
            s1 = _dot(keys_ref[2 * hd], qt_ref[pl.ds(r0, PEER_HALF), :])
            s2 = _dot(keys_ref[2 * hd + 1], qt_ref[pl.ds(r0 + PEER_HALF, PEER_HALF), :])
            rank1, a = _extract_topk(s1, k)
            rank2, b = _extract_topk(s2, k)
            blocks = [a[0:1] + b]
            blocks += [a[r:r + 1] + b[0:8] for r in range(1, 4)]
            blocks += [a[r:r + 1] + b[0:4] for r in range(4, 8)]
            blocks += [a[8:16] + b[0:1]]
            cand = jnp.concatenate(blocks, axis=0)
            crank, _ = _extract_topk(cand, k)
            sel = crank < NOT_SELECTED
            sel_f = jnp.where(sel, 1.0, 0.0)
            row_len = [jnp.sum(sel_f[0:16], axis=0, keepdims=True)]
            row_len += [jnp.sum(sel_f[16 + 8 * r:24 + 8 * r], axis=0, keepdims=True) for r in range(3)]
            row_len += [jnp.sum(sel_f[40 + 4 * r:44 + 4 * r], axis=0, keepdims=True) for r in range(4)]
            row_len += [sel_f[56 + r:57 + r] for r in range(8)]
            top = a[0:1] + b[0:1]
            z = jnp.sum(jnp.where(sel, jnp.exp(cand - top), 0.0), axis=0, keepdims=True)
            lim1 = jnp.zeros(s1.shape, F32)
            for r in range(k):
                lim1 = jnp.where(rank1 == float(r + 1), row_len[r], lim1)
            e1 = jnp.exp(s1 - a[0:1]) / z
            r2 = rank2
            e2 = jnp.exp(s2 - b[0:1])
            for tb in range(n_tb):
                ls = slice(tb * LANES, (tb + 1) * LANES)
                lim1_ref[hd, tb] = lim1[:, ls].reshape(tiles, SUBLANES, LANES)
                e1_ref[hd, tb] = e1[:, ls].reshape(tiles, SUBLANES, LANES)
                r2_ref[hd, tb] = r2[:, ls].reshape(slabs, SLAB_ROWS, LANES)
                e2_ref[hd, tb] = e2[:, ls].reshape(slabs, SLAB_ROWS, LANES)
            return carry

        lax.fori_loop(0, PEER_HEADS, head, 0)

    rows = u_ref.shape[0] // PEER_NKEYS
    row_tiles = rows // SUBLANES
    half = SUBLANES // 2

    def gate_block(idx, carry):
        tile = idx // n_tb
        tb = idx % n_tb
        i_tile = c * row_tiles + tile
        for hf in range(2):
            accs = [[None] * slabs for _ in range(half)]
            for hd in range(PEER_HEADS):
                lim_t = lim1_ref[hd, tb, i_tile]
                e1_t = e1_ref[hd, tb, i_tile]
                lims = [jnp.broadcast_to(lim_t[hf * half + ii:hf * half + ii + 1], (SLAB_ROWS, LANES))
                        for ii in range(half)]
                e1rs = [jnp.broadcast_to(e1_t[hf * half + ii:hf * half + ii + 1], (SLAB_ROWS, LANES))
                        for ii in range(half)]
                for sl in range(slabs):
                    r2 = r2_ref[hd, tb, sl]
                    e2 = e2_ref[hd, tb, sl]
                    for ii in range(half):
                        term = jnp.where(r2 <= lims[ii], e2 * e1rs[ii], 0.0)
                        accs[ii][sl] = term if accs[ii][sl] is None else accs[ii][sl] + term
            for ii in range(half):
                for sl in range(slabs):
                    gate_ref[tb, (tile * SUBLANES + hf * half + ii) * slabs + sl] = accs[ii][sl]
        return carry

    lax.fori_loop(0, row_tiles * n_tb, gate_block, 0)

    h2t = h2t_ref[0]
    chain_rows = PEER_ROWS_PER_CHAIN
    chain_e = chain_rows * PEER_NKEYS
    n_chains = rows // chain_rows

    def expert_pre(ch):
        a_refs[ch % 2][...] = _dot(u_ref[ch * chain_e:(ch + 1) * chain_e, :], h2t)

    def gated(ch):
        for tb in range(n_tb):
            ls = slice(tb * LANES, (tb + 1) * LANES)
            for sb in range(chain_rows * slabs):
                r0 = sb * SLAB_ROWS
                a_blk = a_refs[ch % 2][r0:r0 + SLAB_ROWS, ls]
                gelu = 0.5 * a_blk * (1.0 + lax.erf(a_blk * (2.0 ** -0.5)))
                w_blk = gate_ref[tb, ch * chain_rows * slabs + sb] * gelu
                w_refs[ch % 2][r0:r0 + SLAB_ROWS, ls] = w_blk.astype(MXU_DTYPE)

    def expert_post(ch):
        acc_ref[...] += _dot(vt_ref[:, ch * chain_e:(ch + 1) * chain_e], w_refs[ch % 2][...])

    expert_pre(0)
    for ch in range(n_chains):
        if ch > 0:
            expert_post(ch - 1)
        if ch + 1 < n_chains:
            expert_pre(ch + 1)
        gated(ch)
    expert_post(n_chains - 1)

    @pl.when(c == pl.num_programs(1) - 1)
    def _():
        ffn = acc_ref[...].T
        g2 = mod_ref[0, 5:6, :]
        o_ref[...] = _layer_norm(ALPHA * x1_ref[...] + g2 * ffn, g_ref[...], b_ref[...])


def _peer(h2t, x1, mod, w_peer_q, peer_keys, peer_u, peer_v, ln_g, ln_b, seq, tt):
    n_tok, d = x1.shape
    bsz = n_tok // seq
    rows = PEER_ROWS_PER_CHUNK
    ec = rows * PEER_NKEYS
    n_exp = peer_u.shape[0]
    n_q = w_peer_q.shape[1]
    tiles_per_seq = seq // tt
    full = lambda shape: pl.BlockSpec(shape, lambda t, c: (0,) * len(shape))
    assert rows % SUBLANES == 0 and tt % LANES == 0
    n_tb = tt // LANES
    hk = (PEER_HEADS, n_tb, PEER_NKEYS // SUBLANES, SUBLANES, LANES)
    hk_slab = (PEER_HEADS, n_tb, PEER_NKEYS // SLAB_ROWS, SLAB_ROWS, LANES)
    gate_shape = (n_tb, rows * PEER_NKEYS // SLAB_ROWS, SLAB_ROWS, LANES)
    return pl.pallas_call(
        _peer_kernel,
        grid=(n_tok // tt, n_exp // ec),
        in_specs=[pl.BlockSpec((1, d, tt), lambda t, c: (t // tiles_per_seq, 0, t % tiles_per_seq)),
                  pl.BlockSpec((tt, d), lambda t, c: (t, 0)),
                  pl.BlockSpec((1, 6, d), lambda t, c: (t // tiles_per_seq, 0, 0)),
                  full((n_q, d)),
                  full((2 * PEER_HEADS, PEER_NKEYS, PEER_HALF)),
                  pl.BlockSpec((ec, d), lambda t, c: (c, 0)),
                  pl.BlockSpec((d, ec), lambda t, c: (0, c)),
                  full((1, d)), full((1, d))],
        out_specs=pl.BlockSpec((tt, d), lambda t, c: (t, 0)),
        out_shape=jax.ShapeDtypeStruct((n_tok, d), F32),
        scratch_shapes=[pltpu.VMEM((n_q, tt), MXU_DTYPE),
                        pltpu.VMEM(hk, F32), pltpu.VMEM(hk, F32),
                        pltpu.VMEM(hk_slab, F32), pltpu.VMEM(hk_slab, F32),
                        pltpu.VMEM(gate_shape, F32),
                        pltpu.VMEM((PEER_ROWS_PER_CHAIN * PEER_NKEYS, tt), F32),
                        pltpu.VMEM((PEER_ROWS_PER_CHAIN * PEER_NKEYS, tt), F32),
                        pltpu.VMEM((PEER_ROWS_PER_CHAIN * PEER_NKEYS, tt), MXU_DTYPE),
                        pltpu.VMEM((PEER_ROWS_PER_CHAIN * PEER_NKEYS, tt), MXU_DTYPE),
                        pltpu.VMEM((d, tt), F32)],
        compiler_params=pltpu.CompilerParams(dimension_semantics=("arbitrary", "arbitrary"),
                                             vmem_limit_bytes=VMEM_LIMIT_BYTES),
        name="peer_ln2",
    )(h2t, x1, mod.reshape(bsz, 6, d), w_peer_q.T.astype(MXU_DTYPE),
      peer_keys.reshape(2 * PEER_HEADS, PEER_NKEYS, PEER_HALF).astype(MXU_DTYPE),
      peer_u.astype(MXU_DTYPE), peer_v.T.astype(MXU_DTYPE), ln_g.reshape(1, d), ln_b.reshape(1, d))


def kernel(x, c, positions, w_ada, b_ada, w_in, pool_w, pool_scale, q_norm_g, w_uq, kv_norm_g,
           w_ukv, w_out, ln1_g, ln1_b, w_peer_q, peer_keys, peer_u, peer_v, ln2_g, ln2_b):
    bsz, seq, d = x.shape
    assert w_ada.shape[0] == DEPTH
    tt = min(TOKEN_TILE, seq)
    tq = min(ATTN_Q_BLOCK, seq)
    cw = min(ATTN_CHUNK, tt)
    peer_tt = min(PEER_TOKEN_TILE, seq)
    for l in range(DEPTH):
        mod = _adaln(c, w_ada[l], b_ada[l])
        pool, qt, k, vt = _inproj(x, mod, positions, w_in[l], pool_w[l], pool_scale[l],
                                  q_norm_g[l], w_uq[l], kv_norm_g[l], w_ukv[l], tt, cw)
        att = _attention(qt, k, vt, tq, cw)
        x1, h2 = _outproj(x, pool, att, mod, w_out[l], ln1_g[l], ln1_b[l], tt)
        x = _peer(h2, x1.reshape(bsz * seq, d), mod, w_peer_q[l],
                  peer_keys[l], peer_u[l], peer_v[l], ln2_g[l], ln2_b[l], seq,
                  peer_tt).reshape(bsz, seq, d)
    return x
```

```python
import functools

import jax
import jax.numpy as jnp
from jax import lax
from jax.experimental import pallas as pl
from jax.experimental.pallas import tpu as pltpu

POOL_WINDOWS = (2, 4, 8, 16)
POOL_GROUP = 64
D_POOL = POOL_GROUP * len(POOL_WINDOWS)
V_HEAD = 128
QK_NOPE = 128
QK_ROPE = 64
QK_DIM = QK_NOPE + QK_ROPE
N_HEADS = 6
D_ATT = N_HEADS * V_HEAD
Q_LORA = 384
KV_LORA = 256
ROPE_THETA = 10000.0
PEER_HEADS = 8
PEER_NKEYS = 128
PEER_HALF = 128
PEER_TOPK = 16
DEPTH = 1
ALPHA = (2.0 * DEPTH) ** 0.25
LN_EPS = 1e-5
RMS_EPS = 1e-6
LOG2_E = 1.4426950408889634

LANES = 128
SUBLANES = 8
SLAB_ROWS = 16
VMEM_LIMIT_BYTES = 60000 * 1024

MXU_DTYPE = jnp.bfloat16
F32 = jnp.float32
NOT_SELECTED = 1.0e4

POOL_HALO = 16
TOKEN_TILE = 512
ATTN_Q_BLOCK = 1024
ATTN_CHUNK = 256
PEER_TOKEN_TILE = 512
PEER_ROWS_PER_CHUNK = 16
PEER_ROWS_PER_CHAIN = 8


def _dot(a, b):
    return jnp.dot(a, b, preferred_element_type=F32)


def _dot_nt(a, b):
    return lax.dot_general(a, b, (((1,), (1,)), ((), ())), preferred_element_type=F32)


def _layer_norm(y, g, b):
    mu = jnp.mean(y, axis=-1, keepdims=True)
    d = y - mu
    var = jnp.mean(d * d, axis=-1, keepdims=True)
    return d * lax.rsqrt(var + LN_EPS) * g + b


def _rms_norm(x, g):
    ms = jnp.mean(x * x, axis=-1, keepdims=True)
    return x * lax.rsqrt(ms + RMS_EPS) * g


def _adaln_kernel(c_ref, w_ref, b_ref, o_ref):
    c = c_ref[...]
    c_act = c * jax.nn.sigmoid(c)
    o_ref[...] = _dot(c_act.astype(MXU_DTYPE), w_ref[...].astype(MXU_DTYPE)) + b_ref[...]


def _adaln(c, w_ada, b_ada):
    bsz, d = c.shape
    n = w_ada.shape[1]
    blk = d
    return pl.pallas_call(
        _adaln_kernel,
        grid=(n // blk,),
        in_specs=[pl.BlockSpec((bsz, d), lambda i: (0, 0)),
                  pl.BlockSpec((d, blk), lambda i: (0, i)),
                  pl.BlockSpec((1, blk), lambda i: (0, i))],
        out_specs=pl.BlockSpec((bsz, blk), lambda i: (0, i)),
        out_shape=jax.ShapeDtypeStruct((bsz, n), F32),
        compiler_params=pltpu.CompilerParams(dimension_semantics=("arbitrary",),
                                             vmem_limit_bytes=VMEM_LIMIT_BYTES),
        name="adaln_mod",
    )(c, w_ada, b_ada.reshape(1, n))


def _inproj_kernel(x_ref, mod_ref, pos_ref, invf_ref, win_ref, wpool_ref, pscale_ref, qg_ref,
                   wuqt_ref, kvg_ref, wuk_ref, wuvt_ref, pool_ref, qt_ref, k_ref, vt_ref, ext_ref):
    j = pl.program_id(1)
    tt = x_ref.shape[1]
    x = x_ref[0]
    sh1 = mod_ref[0, 0:1, :]
    sc1 = mod_ref[0, 1:2, :]
    h = x * (1.0 + sc1) + sh1
    z = _dot(h.astype(MXU_DTYPE), win_ref[...])

    @pl.when(j == 0)
    def _():
        ext_ref[0:POOL_HALO, :] = jnp.zeros((POOL_HALO, D_POOL), F32)

    p = z[:, :D_POOL]
    ext_ref[POOL_HALO:POOL_HALO + tt, :] = p

    def shifted(d, col):
        return ext_ref[pl.ds(POOL_HALO - d, tt), col * LANES:(col + 1) * LANES]

    lane = lax.broadcasted_iota(jnp.int32, (tt, LANES), 1)
    low = lane < POOL_GROUP
    t_seq = (j * tt + lax.broadcasted_iota(jnp.int32, (tt, LANES), 0) + 1).astype(F32)
    mixed = []
    for col in range(2):
        w_lo, w_hi = POOL_WINDOWS[2 * col], POOL_WINDOWS[2 * col + 1]
        s_lo = shifted(0, col)
        for d in range(1, w_lo):
            s_lo = s_lo + shifted(d, col)
        s_hi = s_lo
        for d in range(w_lo, w_hi):
            s_hi = s_hi + shifted(d, col)
        win_sum = jnp.where(low, s_lo, s_hi)
        count = jnp.minimum(t_seq, jnp.where(low, float(w_lo), float(w_hi)))
        mixed.append(win_sum / count - p[:, col * LANES:(col + 1) * LANES])
    mixed = jnp.concatenate(mixed, axis=1)
    ext_ref[0:POOL_HALO, :] = ext_ref[tt:tt + POOL_HALO, :]
    pool = _dot(mixed.astype(MXU_DTYPE), wpool_ref[...]) * pscale_ref[...]
    pool_ref[0] = pool.astype(pool_ref.dtype)

    ang_t = invf_ref[...] * pos_ref[0].astype(F32)
    cos_t = jnp.cos(ang_t)
    sin_t = jnp.sin(ang_t)
    cos_t = jnp.concatenate([cos_t, cos_t], axis=0)
    sin_t = jnp.concatenate([sin_t, sin_t], axis=0)

    o1 = D_POOL
    o2 = o1 + Q_LORA
    o3 = o2 + KV_LORA
    cqn = _rms_norm(z[:, o1:o2], qg_ref[...]).astype(MXU_DTYPE)
    qq_t = _dot_nt(wuqt_ref[...], cqn)
    n_nope = N_HEADS * QK_NOPE
    n_rope = N_HEADS * QK_ROPE
    scale = QK_DIM ** -0.5 * LOG2_E
    cw = qt_ref.shape[4]
    for hd in range(N_HEADS):
        r0 = n_nope + hd * QK_ROPE
        rope_t = qq_t[r0:r0 + QK_ROPE] * cos_t + qq_t[r0 + n_rope:r0 + n_rope + QK_ROPE] * sin_t
        qh_t = (jnp.concatenate([qq_t[hd * QK_NOPE:(hd + 1) * QK_NOPE], rope_t], axis=0)
                * scale).astype(qt_ref.dtype)
        for cc in range(tt // cw):
            qt_ref[0, hd, cc] = qh_t[:, cc * cw:(cc + 1) * cw]

    ckvn = _rms_norm(z[:, o2:o3], kvg_ref[...]).astype(MXU_DTYPE)
    k_nope = _dot(ckvn, wuk_ref[...])
    v_t = _dot_nt(wuvt_ref[...], ckvn)
    cs = jnp.concatenate([cos_t, sin_t], axis=0).T
    kk = z[:, o3:] * cs
    k_rope = (kk + pltpu.roll(kk, QK_ROPE, 1))[:, :QK_ROPE]
    for hd in range(N_HEADS):
        kh = jnp.concatenate([k_nope[:, hd * QK_NOPE:(hd + 1) * QK_NOPE], k_rope], axis=1)
        k_ref[0, hd] = kh.astype(k_ref.dtype)
        vt_ref[0, hd, 0] = v_t[hd * V_HEAD:(hd + 1) * V_HEAD].astype(vt_ref.dtype)


def _rotate_half_cols(w):
    half = w.shape[-1] // 2
    return jnp.concatenate([-w[..., half:], w[..., :half]], axis=-1)


def _inproj(x, mod, positions, w_in, pool_w, pool_scale, q_norm_g, w_uq, kv_norm_g, w_ukv, tt, cw):
    bsz, seq, d = x.shape
    o3 = D_POOL + Q_LORA + KV_LORA
    w_in_x = jnp.concatenate([w_in, _rotate_half_cols(w_in[:, o3:])], axis=1).astype(MXU_DTYPE)
    n_groups = len(POOL_WINDOWS)
    w_pool_bd = jnp.zeros((D_POOL, D_POOL), F32)
    for g in range(n_groups):
        w_pool_bd = w_pool_bd.at[g * POOL_GROUP:(g + 1) * POOL_GROUP,
                                 g * POOL_GROUP:(g + 1) * POOL_GROUP].set(pool_w[g])
    w_pool_bd = w_pool_bd.astype(MXU_DTYPE)
    w_uq3 = w_uq.reshape(Q_LORA, N_HEADS, QK_DIM)
    w_q_rope = w_uq3[:, :, QK_NOPE:]
    w_uq_t = jnp.concatenate([w_uq3[:, :, :QK_NOPE].reshape(Q_LORA, N_HEADS * QK_NOPE),
                              w_q_rope.reshape(Q_LORA, N_HEADS * QK_ROPE),
                              _rotate_half_cols(w_q_rope).reshape(Q_LORA, N_HEADS * QK_ROPE)],
                             axis=1).T.astype(MXU_DTYPE)
    w_ukv3 = w_ukv.reshape(KV_LORA, N_HEADS, QK_NOPE + V_HEAD)
    w_uk = w_ukv3[:, :, :QK_NOPE].reshape(KV_LORA, N_HEADS * QK_NOPE).astype(MXU_DTYPE)
    w_uv_t = w_ukv3[:, :, QK_NOPE:].reshape(KV_LORA, N_HEADS * V_HEAD).T.astype(MXU_DTYPE)
    inv_freq = ROPE_THETA ** (-jnp.arange(0, QK_ROPE, 2, dtype=F32) / QK_ROPE)
    invf = inv_freq.reshape(QK_ROPE // 2, 1)
    full = lambda shape: pl.BlockSpec(shape, lambda b, j: (0,) * len(shape))
    return pl.pallas_call(
        _inproj_kernel,
        grid=(bsz, seq // tt),
        in_specs=[pl.BlockSpec((1, tt, d), lambda b, j: (b, j, 0)),
                  pl.BlockSpec((1, 6, d), lambda b, j: (b, 0, 0)),
                  pl.BlockSpec((1, 1, tt), lambda b, j: (b, 0, j)),
                  full(invf.shape),
                  full(w_in_x.shape),
                  full((D_POOL, D_POOL)),
                  full((1, D_POOL)),
                  full((1, Q_LORA)),
                  full(w_uq_t.shape),
                  full((1, KV_LORA)),
                  full(w_uk.shape),
                  full(w_uv_t.shape)],
        out_specs=[pl.BlockSpec((1, tt, D_POOL), lambda b, j: (b, j, 0)),
                   pl.BlockSpec((1, N_HEADS, tt // cw, QK_DIM, cw), lambda b, j: (b, 0, j, 0, 0)),
                   pl.BlockSpec((1, N_HEADS, tt, QK_DIM), lambda b, j: (b, 0, j, 0)),
                   pl.BlockSpec((1, N_HEADS, 1, V_HEAD, tt), lambda b, j: (b, 0, j, 0, 0))],
        out_shape=[jax.ShapeDtypeStruct((bsz, seq, D_POOL), MXU_DTYPE),
                   jax.ShapeDtypeStruct((bsz, N_HEADS, seq // cw, QK_DIM, cw), MXU_DTYPE),
                   jax.ShapeDtypeStruct((bsz, N_HEADS, seq, QK_DIM), MXU_DTYPE),
                   jax.ShapeDtypeStruct((bsz, N_HEADS, seq // tt, V_HEAD, tt), MXU_DTYPE)],
        scratch_shapes=[pltpu.VMEM((tt + POOL_HALO, D_POOL), F32)],
        compiler_params=pltpu.CompilerParams(dimension_semantics=("arbitrary", "arbitrary"),
                                             vmem_limit_bytes=VMEM_LIMIT_BYTES),
        name="in_proj",
    )(x, mod.reshape(bsz, 6, d), positions.reshape(bsz, 1, seq), invf, w_in_x, w_pool_bd,
      pool_scale.reshape(1, D_POOL), q_norm_g.reshape(1, Q_LORA), w_uq_t,
      kv_norm_g.reshape(1, KV_LORA), w_uk, w_uv_t)


def _attn_kernel(qt_ref, k_ref, vt_ref, o_ref, sa_ref, sb_ref, m_ref, l_ref, acc_ref, *, tq, tk, cw):
    seq = k_ref.shape[2]
    n_chunks = tq // cw
    assert tq == 2 * tk

    def q_block(qi, carry):
        q0 = pl.multiple_of(qi * tq, tq)
        m_ref[...] = jnp.full(m_ref.shape, -jnp.inf, F32)
        l_ref[...] = jnp.zeros(l_ref.shape, F32)
        acc_ref[...] = jnp.zeros(acc_ref.shape, F32)

        def active_chunks(rel):
            return [c for c in range(n_chunks) if rel is None or (c + 1) * cw > rel * tk]

        def scores_into(buf_ref, kj, rel):
            k0 = pl.multiple_of(kj * tk, tk)
            k = k_ref[0, 0, pl.ds(k0, tk), :]
            for c in active_chunks(rel):
                buf_ref[c] = _dot(k, qt_ref[0, 0, qi * n_chunks + c])

        def consume(buf_ref, kj, rel):
            vt = vt_ref[0, 0, kj]
            probs, alphas = {}, {}
            for c in active_chunks(rel):
                cols = slice(c * cw, (c + 1) * cw)
                s = buf_ref[c]
                if rel is not None and c * cw < (rel + 1) * tk:
                    kv_idx = rel * tk + lax.broadcasted_iota(jnp.int32, (tk, cw), 0)
                    q_idx = c * cw + lax.broadcasted_iota(jnp.int32, (tk, cw), 1)
                    s = jnp.where(q_idx >= kv_idx, s, -jnp.inf)
                m_prev = m_ref[:, cols]
                m_new = jnp.maximum(m_prev, jnp.max(s, axis=0, keepdims=True))
                alphas[c] = jnp.exp2(m_prev - m_new)
                p = jnp.exp2(s - m_new)
                l_ref[:, cols] = alphas[c] * l_ref[:, cols] + jnp.sum(p, axis=0, keepdims=True)
                m_ref[:, cols] = m_new
                probs[c] = p.astype(MXU_DTYPE)
            for c in active_chunks(rel):
                cols = slice(c * cw, (c + 1) * cw)
                acc_ref[:, cols] = alphas[c] * acc_ref[:, cols] + _dot(vt, probs[c])

        def pair(jj, c):
            scores_into(sb_ref, 2 * jj + 1, None)
            consume(sa_ref, 2 * jj, None)
            scores_into(sa_ref, 2 * jj + 2, None)
            consume(sb_ref, 2 * jj + 1, None)
            return c

        scores_into(sa_ref, 0, None)
        lax.fori_loop(0, qi, pair, 0)
        scores_into(sb_ref, 2 * qi + 1, 1)
        consume(sa_ref, 2 * qi, 0)
        consume(sb_ref, 2 * qi + 1, 1)
        out_t = acc_ref[...] / l_ref[...]
        o_ref[0, pl.ds(q0, tq), :] = out_t.T.astype(o_ref.dtype)
        return carry

    lax.fori_loop(0, seq // tq, q_block, 0)


def _attention(qt, k, vt, tq, cw):
    bsz, nh, seq, _ = k.shape
    tk = vt.shape[4]
    return pl.pallas_call(
        functools.partial(_attn_kernel, tq=tq, tk=tk, cw=cw),
        grid=(bsz, nh),
        in_specs=[pl.BlockSpec((1, 1, seq // cw, QK_DIM, cw), lambda b, h: (b, h, 0, 0, 0)),
                  pl.BlockSpec((1, 1, seq, QK_DIM), lambda b, h: (b, h, 0, 0)),
                  pl.BlockSpec((1, 1, seq // tk, V_HEAD, tk), lambda b, h: (b, h, 0, 0, 0))],
        out_specs=pl.BlockSpec((1, seq, V_HEAD), lambda b, h: (b, 0, h)),
        out_shape=jax.ShapeDtypeStruct((bsz, seq, nh * V_HEAD), MXU_DTYPE),
        scratch_shapes=[pltpu.VMEM((tq // cw, tk, cw), F32),
                        pltpu.VMEM((tq // cw, tk, cw), F32),
                        pltpu.VMEM((1, tq), F32),
                        pltpu.VMEM((1, tq), F32),
                        pltpu.VMEM((V_HEAD, tq), F32)],
        compiler_params=pltpu.CompilerParams(dimension_semantics=("arbitrary", "arbitrary"),
                                             vmem_limit_bytes=VMEM_LIMIT_BYTES),
        name="mla_attention",
    )(qt, k, vt)


def _outproj_kernel(x_ref, pool_ref, att_ref, mod_ref, wout_ref, g_ref, b_ref, x1_ref, h2t_ref):
    mix = _dot(pool_ref[0], wout_ref[0:D_POOL, :]) + _dot(att_ref[0], wout_ref[D_POOL:, :])
    g1 = mod_ref[0, 2:3, :]
    sh2 = mod_ref[0, 3:4, :]
    sc2 = mod_ref[0, 4:5, :]
    x1 = _layer_norm(ALPHA * x_ref[0] + g1 * mix, g_ref[...], b_ref[...])
    x1_ref[0] = x1
    h2t_ref[0] = (x1 * (1.0 + sc2) + sh2).T.astype(h2t_ref.dtype)


def _outproj(x, pool, att, mod, w_out, ln_g, ln_b, tt):
    bsz, seq, d = x.shape
    tile = lambda n: pl.BlockSpec((1, tt, n), lambda b, j: (b, j, 0))
    full = lambda shape: pl.BlockSpec(shape, lambda b, j: (0,) * len(shape))
    return pl.pallas_call(
        _outproj_kernel,
        grid=(bsz, seq // tt),
        in_specs=[tile(d), tile(D_POOL), tile(D_ATT),
                  pl.BlockSpec((1, 6, d), lambda b, j: (b, 0, 0)),
                  full(w_out.shape), full((1, d)), full((1, d))],
        out_specs=[tile(d), pl.BlockSpec((1, d, tt), lambda b, j: (b, 0, j))],
        out_shape=[jax.ShapeDtypeStruct((bsz, seq, d), F32),
                   jax.ShapeDtypeStruct((bsz, d, seq), MXU_DTYPE)],
        compiler_params=pltpu.CompilerParams(dimension_semantics=("arbitrary", "arbitrary"),
                                             vmem_limit_bytes=VMEM_LIMIT_BYTES),
        name="out_proj_ln1",
    )(x, pool, att, mod.reshape(bsz, 6, d), w_out.astype(MXU_DTYPE), ln_g.reshape(1, d),
      ln_b.reshape(1, d))


def _extract_topk(s, k, with_rank=True):
    n = s.shape[0]
    iota = lax.broadcasted_iota(jnp.int32, s.shape, 0).astype(F32)
    rank = jnp.full(s.shape, NOT_SELECTED, F32)
    vals = []
    sub = lax.broadcasted_iota(jnp.int32, (SUBLANES, s.shape[1]), 0).astype(F32)
    for r in range(k):
        nodes = [(s[g:g + SUBLANES], float(g)) for g in range(0, n, SUBLANES)]
        while len(nodes) > 1:
            merged = []
            for a in range(0, len(nodes) - 1, 2):
                (va, ga), (vb, gb) = nodes[a], nodes[a + 1]
                take_b = vb > va
                merged.append((jnp.where(take_b, vb, va), jnp.where(take_b, gb, ga)))
            if len(nodes) % 2:
                merged.append(nodes[-1])
            nodes = merged
        v8, g8 = nodes[0]
        m = jnp.max(v8, axis=0, keepdims=True)
        first = jnp.min(jnp.where(v8 == m, g8 + sub, float(n)), axis=0, keepdims=True)
        sel = iota == first
        if with_rank:
            rank = jnp.where(sel, float(r + 1), rank)
        s = jnp.where(sel, -jnp.inf, s)
        vals.append(m)
    return (rank if with_rank else s == -jnp.inf), jnp.concatenate(vals, axis=0)


def _peer_kernel(h2t_ref, x1_ref, mod_ref, wq_ref, keys_ref, u_ref, vt_ref, g_ref, b_ref, o_ref,
                 qt_ref, lim1_ref, e1_ref, r2_ref, e2_ref, gate_ref, a0_ref, a1_ref, w0_ref, w1_ref,
                 acc_ref):
    a_refs = (a0_ref, a1_ref)
    w_refs = (w0_ref, w1_ref)
    c = pl.program_id(1)
    tt = h2t_ref.shape[2]
    k = PEER_TOPK
    slabs = PEER_NKEYS // SLAB_ROWS
    tiles = PEER_NKEYS // SUBLANES
    n_tb = tt // LANES

    @pl.when(c == 0)
    def _():
        acc_ref[...] = jnp.zeros(acc_ref.shape, F32)
        qt_ref[...] = _dot(wq_ref[...], h2t_ref[0]).astype(qt_ref.dtype)

        def head(hd, carry):
            r0 = pl.multiple_of(hd * 2 * PEER_HALF, 2 * PEER_HALF)
            s1 = _dot(keys_ref[2 * hd], qt_ref[pl.ds(r0, PEER_HALF), :])
            s2 = _dot(keys_ref[2 * hd + 1], qt_ref[pl.ds(r0 + PEER_HALF, PEER_HALF), :])
            rank1, a = _extract_topk(s1, k)
            rank2, b = _extract_topk(s2, k)
            blocks = [a[0:1] + b]
            blocks += [a[r:r + 1] + b[0:8] for r in range(1, 4)]
            blocks += [a[r:r + 1] + b[0:4] for r in range(4, 8)]
            blocks += [a[8:16] + b[0:1]]
            cand = jnp.concatenate(blocks, axis=0)
            sel, _ = _extract_topk(cand, k, with_rank=False)
            sel_f = jnp.where(sel, 1.0, 0.0)
            row_len = [jnp.sum(sel_f[0:16], axis=0, keepdims=True)]
            row_len += [jnp.sum(sel_f[16 + 8 * r:24 + 8 * r], axis=0, keepdims=True) for r in range(3)]
            row_len += [jnp.sum(sel_f[40 + 4 * r:44 + 4 * r], axis=0, keepdims=True) for r in range(4)]
            row_len += [sel_f[56 + r:57 + r] for r in range(8)]
            top = a[0:1] + b[0:1]
            z = jnp.sum(jnp.where(sel, jnp.exp(cand - top), 0.0), axis=0, keepdims=True)
            n_tail = jnp.sum(sel_f[56:64], axis=0, keepdims=True)
            lim1 = jnp.where(rank1 <= 8.0 + n_tail, 1.0, 0.0)
            for r in range(8):
                lim1 = jnp.where(rank1 == float(r + 1), row_len[r], lim1)
            e1 = jnp.exp(s1 - a[0:1]) / z
            r2 = rank2
            e2 = jnp.exp(s2 - b[0:1])
            for tb in range(n_tb):
                ls = slice(tb * LANES, (tb + 1) * LANES)
                lim1_ref[hd, tb] = lim1[:, ls].reshape(tiles, SUBLANES, LANES)
                e1_ref[hd, tb] = e1[:, ls].reshape(tiles, SUBLANES, LANES)
                r2_ref[hd, tb] = r2[:, ls].reshape(slabs, SLAB_ROWS, LANES)
                e2_ref[hd, tb] = e2[:, ls].reshape(slabs, SLAB_ROWS, LANES)
            return carry

        lax.fori_loop(0, PEER_HEADS, head, 0)

    rows = u_ref.shape[0] // PEER_NKEYS
    row_tiles = rows // SUBLANES
    half = SUBLANES // 2

    def gate_block(idx, carry):
        tile = idx // n_tb
        tb = idx % n_tb
        i_tile = c * row_tiles + tile
        for hf in range(2):
            accs = [[None] * slabs for _ in range(half)]
            for hd in range(PEER_HEADS):
                lim_t = lim1_ref[hd, tb, i_tile]
                e1_t = e1_ref[hd, tb, i_tile]
                lims = [jnp.broadcast_to(lim_t[hf * half + ii:hf * half + ii + 1], (SLAB_ROWS, LANES))
                        for ii in range(half)]
                e1rs = [jnp.broadcast_to(e1_t[hf * half + ii:hf * half + ii + 1], (SLAB_ROWS, LANES))
                        for ii in range(half)]
                for sl in range(slabs):
                    r2 = r2_ref[hd, tb, sl]
                    e2 = e2_ref[hd, tb, sl]
                    for ii in range(half):
                        term = jnp.where(r2 <= lims[ii], e2 * e1rs[ii], 0.0)
                        accs[ii][sl] = term if accs[ii][sl] is None else accs[ii][sl] + term
            for ii in range(half):
                for sl in range(slabs):
                    gate_ref[tb, (tile * SUBLANES + hf * half + ii) * slabs + sl] = accs[ii][sl]
        return carry

    lax.fori_loop(0, row_tiles * n_tb, gate_block, 0)

    h2t = h2t_ref[0]
    chain_rows = PEER_ROWS_PER_CHAIN
    chain_e = chain_rows * PEER_NKEYS
    n_chains = rows // chain_rows

    def expert_pre(ch):
        a_refs[ch % 2][...] = _dot(u_ref[ch * chain_e:(ch + 1) * chain_e, :], h2t)

    def gated(ch):
        for tb in range(n_tb):
            ls = slice(tb * LANES, (tb + 1) * LANES)
            for sb in range(chain_rows * slabs):
                r0 = sb * SLAB_ROWS
                a_blk = a_refs[ch % 2][r0:r0 + SLAB_ROWS, ls]
                gelu = 0.5 * a_blk * (1.0 + lax.erf(a_blk * (2.0 ** -0.5)))
                w_blk = gate_ref[tb, ch * chain_rows * slabs + sb] * gelu
                w_refs[ch % 2][r0:r0 + SLAB_ROWS, ls] = w_blk.astype(MXU_DTYPE)

    def expert_post(ch):
        acc_ref[...] += _dot(vt_ref[:, ch * chain_e:(ch + 1) * chain_e], w_refs[ch % 2][...])

    expert_pre(0)
    for ch in range(n_chains):
        if ch > 0:
            expert_post(ch - 1)
        if ch + 1 < n_chains:
            expert_pre(ch + 1)
        gated(ch)
    expert_post(n_chains - 1)

    @pl.when(c == pl.num_programs(1) - 1)
    def _():
        ffn = acc_ref[...].T
        g2 = mod_ref[0, 5:6, :]
        o_ref[...] = _layer_norm(ALPHA * x1_ref[...] + g2 * ffn, g_ref[...], b_ref[...])


def _peer(h2t, x1, mod, w_peer_q, peer_keys, peer_u, peer_v, ln_g, ln_b, seq, tt):
    n_tok, d = x1.shape
    bsz = n_tok // seq
    rows = PEER_ROWS_PER_CHUNK
    ec = rows * PEER_NKEYS
    n_exp = peer_u.shape[0]
    n_q = w_peer_q.shape[1]
    tiles_per_seq = seq // tt
    full = lambda shape: pl.BlockSpec(shape, lambda t, c: (0,) * len(shape))
    assert rows % SUBLANES == 0 and tt % LANES == 0
    n_tb = tt // LANES
    hk = (PEER_HEADS, n_tb, PEER_NKEYS // SUBLANES, SUBLANES, LANES)
    hk_slab = (PEER_HEADS, n_tb, PEER_NKEYS // SLAB_ROWS, SLAB_ROWS, LANES)
    gate_shape = (n_tb, rows * PEER_NKEYS // SLAB_ROWS, SLAB_ROWS, LANES)
    return pl.pallas_call(
        _peer_kernel,
        grid=(n_tok // tt, n_exp // ec),
        in_specs=[pl.BlockSpec((1, d, tt), lambda t, c: (t // tiles_per_seq, 0, t % tiles_per_seq)),
                  pl.BlockSpec((tt, d), lambda t, c: (t, 0)),
                  pl.BlockSpec((1, 6, d), lambda t, c: (t // tiles_per_seq, 0, 0)),
                  full((n_q, d)),
                  full((2 * PEER_HEADS, PEER_NKEYS, PEER_HALF)),
                  pl.BlockSpec((ec, d), lambda t, c: (c, 0)),
                  pl.BlockSpec((d, ec), lambda t, c: (0, c)),
                  full((1, d)), full((1, d))],
        out_specs=pl.BlockSpec((tt, d), lambda t, c: (t, 0)),
        out_shape=jax.ShapeDtypeStruct((n_tok, d), F32),
        scratch_shapes=[pltpu.VMEM((n_q, tt), MXU_DTYPE),
                        pltpu.VMEM(hk, F32), pltpu.VMEM(hk, F32),
                        pltpu.VMEM(hk_slab, F32), pltpu.VMEM(hk_slab, F32),
                        pltpu.VMEM(gate_shape, F32),
                        pltpu.VMEM((PEER_ROWS_PER_CHAIN * PEER_NKEYS, tt), F32),
                        pltpu.VMEM((PEER_ROWS_PER_CHAIN * PEER_NKEYS, tt), F32),
                        pltpu.VMEM((PEER_ROWS_PER_CHAIN * PEER_NKEYS, tt), MXU_DTYPE),
                        pltpu.VMEM((PEER_ROWS_PER_CHAIN * PEER_NKEYS, tt), MXU_DTYPE),
                        pltpu.VMEM((d, tt), F32)],
        compiler_params=pltpu.CompilerParams(dimension_semantics=("arbitrary", "arbitrary"),
                                             vmem_limit_bytes=VMEM_LIMIT_BYTES),
        name="peer_ln2",
    )(h2t, x1, mod.reshape(bsz, 6, d), w_peer_q.T.astype(MXU_DTYPE),
      peer_keys.reshape(2 * PEER_HEADS, PEER_NKEYS, PEER_HALF).astype(MXU_DTYPE),
      peer_u.astype(MXU_DTYPE), peer_v.T.astype(MXU_DTYPE), ln_g.reshape(1, d), ln_b.reshape(1, d))


def kernel(x, c, positions, w_ada, b_ada, w_in, pool_w, pool_scale, q_norm_g, w_uq, kv_norm_g,
           w_ukv, w_out, ln1_g, ln1_b, w_peer_q, peer_keys, peer_u, peer_v, ln2_g, ln2_b):
    bsz, seq, d = x.shape
    assert w_ada.shape[0] == DEPTH
    tt = min(TOKEN_TILE, seq)
    tq = min(ATTN_Q_BLOCK, seq)
    cw = min(ATTN_CHUNK, tt)
    peer_tt = min(PEER_TOKEN_TILE, seq)
    for l in range(DEPTH):
        mod = _adaln(c, w_ada[l], b_ada[l])
        pool, qt, k, vt = _inproj(x, mod, positions, w_in[l], pool_w[l], pool_scale[l],
                                  q_norm_g[l], w_uq[l], kv_norm_g[l], w_ukv[l], tt, cw)
        att = _attention(qt, k, vt, tq, cw)
        x1, h2 = _outproj(x, pool, att, mod, w_out[l], ln1_g[l], ln1_b[l], tt)
        x = _peer(h2, x1.reshape(bsz * seq, d), mod, w_peer_q[l],
                  peer_keys[l], peer_u[l], peer_v[l], ln2_g[l], ln2_b[l], seq,
                  peer_tt).reshape(bsz, seq, d)
    return x
```

```python
import functools

import jax
import jax.numpy as jnp
from jax import lax
from jax.experimental import pallas as pl
from jax.experimental.pallas import tpu as pltpu

POOL_WINDOWS = (2, 4, 8, 16)
POOL_GROUP = 64
D_POOL = POOL_GROUP * len(POOL_WINDOWS)
V_HEAD = 128
QK_NOPE = 128
QK_ROPE = 64
QK_DIM = QK_NOPE + QK_ROPE
N_HEADS = 6
D_ATT = N_HEADS * V_HEAD
Q_LORA = 384
KV_LORA = 256
ROPE_THETA = 10000.0
PEER_HEADS = 8
PEER_NKEYS = 128
PEER_HALF = 128
PEER_TOPK = 16
DEPTH = 1
ALPHA = (2.0 * DEPTH) ** 0.25
LN_EPS = 1e-5
RMS_EPS = 1e-6
LOG2_E = 1.4426950408889634

LANES = 128
SUBLANES = 8
SLAB_ROWS = 16
VMEM_LIMIT_BYTES = 60000 * 1024

MXU_DTYPE = jnp.bfloat16
F32 = jnp.float32
NOT_SELECTED = 1.0e4

POOL_HALO = 16
TOKEN_TILE = 512
ATTN_Q_BLOCK = 1024
ATTN_CHUNK = 256
PEER_TOKEN_TILE = 512
PEER_ROWS_PER_CHUNK = 16
PEER_ROWS_PER_CHAIN = 8


def _dot(a, b):
    return jnp.dot(a, b, preferred_element_type=F32)


def _dot_nt(a, b):
    return lax.dot_general(a, b, (((1,), (1,)), ((), ())), preferred_element_type=F32)


def _layer_norm(y, g, b):
    mu = jnp.mean(y, axis=-1, keepdims=True)
    d = y - mu
    var = jnp.mean(d * d, axis=-1, keepdims=True)
    return d * lax.rsqrt(var + LN_EPS) * g + b


def _rms_norm(x, g):
    ms = jnp.mean(x * x, axis=-1, keepdims=True)
    return x * lax.rsqrt(ms + RMS_EPS) * g


def _adaln_kernel(c_ref, w_ref, b_ref, o_ref):
    c = c_ref[...]
    c_act = c * jax.nn.sigmoid(c)
    o_ref[...] = _dot(c_act.astype(MXU_DTYPE), w_ref[...].astype(MXU_DTYPE)) + b_ref[...]


def _adaln(c, w_ada, b_ada):
    bsz, d = c.shape
    n = w_ada.shape[1]
    blk = d
    return pl.pallas_call(
        _adaln_kernel,
        grid=(n // blk,),
        in_specs=[pl.BlockSpec((bsz, d), lambda i: (0, 0)),
                  pl.BlockSpec((d, blk), lambda i: (0, i)),
                  pl.BlockSpec((1, blk), lambda i: (0, i))],
        out_specs=pl.BlockSpec((bsz, blk), lambda i: (0, i)),
        out_shape=jax.ShapeDtypeStruct((bsz, n), F32),
        compiler_params=pltpu.CompilerParams(dimension_semantics=("arbitrary",),
                                             vmem_limit_bytes=VMEM_LIMIT_BYTES),
        name="adaln_mod",
    )(c, w_ada, b_ada.reshape(1, n))


def _inproj_kernel(x_ref, mod_ref, pos_ref, invf_ref, win_ref, wpool_ref, pscale_ref, qg_ref,
                   wuqt_ref, kvg_ref, wuk_ref, wuvt_ref, pool_ref, qt_ref, k_ref, vt_ref, ext_ref):
    j = pl.program_id(1)
    tt = x_ref.shape[1]
    x = x_ref[0]
    sh1 = mod_ref[0, 0:1, :]
    sc1 = mod_ref[0, 1:2, :]
    h = x * (1.0 + sc1) + sh1
    z = _dot(h.astype(MXU_DTYPE), win_ref[...])

    @pl.when(j == 0)
    def _():
        ext_ref[0:POOL_HALO, :] = jnp.zeros((POOL_HALO, D_POOL), F32)

    p = z[:, :D_POOL]
    ext_ref[POOL_HALO:POOL_HALO + tt, :] = p

    def shifted(d, col):
        return ext_ref[pl.ds(POOL_HALO - d, tt), col * LANES:(col + 1) * LANES]

    lane = lax.broadcasted_iota(jnp.int32, (tt, LANES), 1)
    low = lane < POOL_GROUP
    t_seq = (j * tt + lax.broadcasted_iota(jnp.int32, (tt, LANES), 0) + 1).astype(F32)
    mixed = []
    for col in range(2):
        w_lo, w_hi = POOL_WINDOWS[2 * col], POOL_WINDOWS[2 * col + 1]
        s_lo = shifted(0, col)
        for d in range(1, w_lo):
            s_lo = s_lo + shifted(d, col)
        s_hi = s_lo
        for d in range(w_lo, w_hi):
            s_hi = s_hi + shifted(d, col)
        win_sum = jnp.where(low, s_lo, s_hi)
        count = jnp.minimum(t_seq, jnp.where(low, float(w_lo), float(w_hi)))
        mixed.append(win_sum / count - p[:, col * LANES:(col + 1) * LANES])
    mixed = jnp.concatenate(mixed, axis=1)
    ext_ref[0:POOL_HALO, :] = ext_ref[tt:tt + POOL_HALO, :]
    pool = _dot(mixed.astype(MXU_DTYPE), wpool_ref[...]) * pscale_ref[...]
    pool_ref[0] = pool.astype(pool_ref.dtype)

    ang_t = invf_ref[...] * pos_ref[0].astype(F32)
    cos_t = jnp.cos(ang_t)
    sin_t = jnp.sin(ang_t)
    cos_t = jnp.concatenate([cos_t, cos_t], axis=0)
    sin_t = jnp.concatenate([sin_t, sin_t], axis=0)

    o1 = D_POOL
    o2 = o1 + Q_LORA
    o3 = o2 + KV_LORA
    cqn = _rms_norm(z[:, o1:o2], qg_ref[...]).astype(MXU_DTYPE)
    qq_t = _dot_nt(wuqt_ref[...], cqn)
    n_nope = N_HEADS * QK_NOPE
    n_rope = N_HEADS * QK_ROPE
    scale = QK_DIM ** -0.5 * LOG2_E
    cw = qt_ref.shape[4]
    for hd in range(N_HEADS):
        r0 = n_nope + hd * QK_ROPE
        rope_t = qq_t[r0:r0 + QK_ROPE] * cos_t + qq_t[r0 + n_rope:r0 + n_rope + QK_ROPE] * sin_t
        qh_t = (jnp.concatenate([qq_t[hd * QK_NOPE:(hd + 1) * QK_NOPE], rope_t], axis=0)
                * scale).astype(qt_ref.dtype)
        for cc in range(tt // cw):
            qt_ref[0, hd, cc] = qh_t[:, cc * cw:(cc + 1) * cw]

    ckvn = _rms_norm(z[:, o2:o3], kvg_ref[...]).astype(MXU_DTYPE)
    k_nope = _dot(ckvn, wuk_ref[...])
    v_t = _dot_nt(wuvt_ref[...], ckvn)
    cs = jnp.concatenate([cos_t, sin_t], axis=0).T
    kk = z[:, o3:] * cs
    k_rope = (kk + pltpu.roll(kk, QK_ROPE, 1))[:, :QK_ROPE]
    for hd in range(N_HEADS):
        kh = jnp.concatenate([k_nope[:, hd * QK_NOPE:(hd + 1) * QK_NOPE], k_rope], axis=1)
        k_ref[0, hd] = kh.astype(k_ref.dtype)
        vt_ref[0, hd, 0] = v_t[hd * V_HEAD:(hd + 1) * V_HEAD].astype(vt_ref.dtype)


def _rotate_half_cols(w):
    half = w.shape[-1] // 2
    return jnp.concatenate([-w[..., half:], w[..., :half]], axis=-1)


def _inproj(x, mod, positions, w_in, pool_w, pool_scale, q_norm_g, w_uq, kv_norm_g, w_ukv, tt, cw):
    bsz, seq, d = x.shape
    o3 = D_POOL + Q_LORA + KV_LORA
    w_in_x = jnp.concatenate([w_in, _rotate_half_cols(w_in[:, o3:])], axis=1).astype(MXU_DTYPE)
    n_groups = len(POOL_WINDOWS)
    w_pool_bd = jnp.zeros((D_POOL, D_POOL), F32)
    for g in range(n_groups):
        w_pool_bd = w_pool_bd.at[g * POOL_GROUP:(g + 1) * POOL_GROUP,
                                 g * POOL_GROUP:(g + 1) * POOL_GROUP].set(pool_w[g])
    w_pool_bd = w_pool_bd.astype(MXU_DTYPE)
    w_uq3 = w_uq.reshape(Q_LORA, N_HEADS, QK_DIM)
    w_q_rope = w_uq3[:, :, QK_NOPE:]
    w_uq_t = jnp.concatenate([w_uq3[:, :, :QK_NOPE].reshape(Q_LORA, N_HEADS * QK_NOPE),
                              w_q_rope.reshape(Q_LORA, N_HEADS * QK_ROPE),
                              _rotate_half_cols(w_q_rope).reshape(Q_LORA, N_HEADS * QK_ROPE)],
                             axis=1).T.astype(MXU_DTYPE)
    w_ukv3 = w_ukv.reshape(KV_LORA, N_HEADS, QK_NOPE + V_HEAD)
    w_uk = w_ukv3[:, :, :QK_NOPE].reshape(KV_LORA, N_HEADS * QK_NOPE).astype(MXU_DTYPE)
    w_uv_t = w_ukv3[:, :, QK_NOPE:].reshape(KV_LORA, N_HEADS * V_HEAD).T.astype(MXU_DTYPE)
    inv_freq = ROPE_THETA ** (-jnp.arange(0, QK_ROPE, 2, dtype=F32) / QK_ROPE)
    invf = inv_freq.reshape(QK_ROPE // 2, 1)
    full = lambda shape: pl.BlockSpec(shape, lambda b, j: (0,) * len(shape))
    return pl.pallas_call(
        _inproj_kernel,
        grid=(bsz, seq // tt),
        in_specs=[pl.BlockSpec((1, tt, d), lambda b, j: (b, j, 0)),
                  pl.BlockSpec((1, 6, d), lambda b, j: (b, 0, 0)),
                  pl.BlockSpec((1, 1, tt), lambda b, j: (b, 0, j)),
                  full(invf.shape),
                  full(w_in_x.shape),
                  full((D_POOL, D_POOL)),
                  full((1, D_POOL)),
                  full((1, Q_LORA)),
                  full(w_uq_t.shape),
                  full((1, KV_LORA)),
                  full(w_uk.shape),
                  full(w_uv_t.shape)],
        out_specs=[pl.BlockSpec((1, tt, D_POOL), lambda b, j: (b, j, 0)),
                   pl.BlockSpec((1, N_HEADS, tt // cw, QK_DIM, cw), lambda b, j: (b, 0, j, 0, 0)),
                   pl.BlockSpec((1, N_HEADS, tt, QK_DIM), lambda b, j: (b, 0, j, 0)),
                   pl.BlockSpec((1, N_HEADS, 1, V_HEAD, tt), lambda b, j: (b, 0, j, 0, 0))],
        out_shape=[jax.ShapeDtypeStruct((bsz, seq, D_POOL), MXU_DTYPE),
                   jax.ShapeDtypeStruct((bsz, N_HEADS, seq // cw, QK_DIM, cw), MXU_DTYPE),
                   jax.ShapeDtypeStruct((bsz, N_HEADS, seq, QK_DIM), MXU_DTYPE),
                   jax.ShapeDtypeStruct((bsz, N_HEADS, seq // tt, V_HEAD, tt), MXU_DTYPE)],
        scratch_shapes=[pltpu.VMEM((tt + POOL_HALO, D_POOL), F32)],
        compiler_params=pltpu.CompilerParams(dimension_semantics=("arbitrary", "arbitrary"),
                                             vmem_limit_bytes=VMEM_LIMIT_BYTES),
        name="in_proj",
    )(x, mod.reshape(bsz, 6, d), positions.reshape(bsz, 1, seq), invf, w_in_x, w_pool_bd,
      pool_scale.reshape(1, D_POOL), q_norm_g.reshape(1, Q_LORA), w_uq_t,
      kv_norm_g.reshape(1, KV_LORA), w_uk, w_uv_t)


def _attn_kernel(qt_ref, k_ref, vt_ref, o_ref, sa_ref, sb_ref, m_ref, l_ref, acc_ref, *, tq, tk, cw):
    seq = k_ref.shape[2]
    n_chunks = tq // cw
    assert tq == 2 * tk

    def q_block(qi, carry):
        q0 = pl.multiple_of(qi * tq, tq)
        m_ref[...] = jnp.full(m_ref.shape, -jnp.inf, F32)
        l_ref[...] = jnp.zeros(l_ref.shape, F32)
        acc_ref[...] = jnp.zeros(acc_ref.shape, F32)

        def active_chunks(rel):
            return [c for c in range(n_chunks) if rel is None or (c + 1) * cw > rel * tk]

        def scores_into(buf_ref, kj, rel):
            k0 = pl.multiple_of(kj * tk, tk)
            k = k_ref[0, 0, pl.ds(k0, tk), :]
            for c in active_chunks(rel):
                buf_ref[c] = _dot(k, qt_ref[0, 0, qi * n_chunks + c])

        def consume(buf_ref, kj, rel):
            vt = vt_ref[0, 0, kj]
            probs, alphas = {}, {}
            for c in active_chunks(rel):
                cols = slice(c * cw, (c + 1) * cw)
                s = buf_ref[c]
                if rel is not None and c * cw < (rel + 1) * tk:
                    kv_idx = rel * tk + lax.broadcasted_iota(jnp.int32, (tk, cw), 0)
                    q_idx = c * cw + lax.broadcasted_iota(jnp.int32, (tk, cw), 1)
                    s = jnp.where(q_idx >= kv_idx, s, -jnp.inf)
                m_prev = m_ref[:, cols]
                m_new = jnp.maximum(m_prev, jnp.max(s, axis=0, keepdims=True))
                alphas[c] = jnp.exp2(m_prev - m_new)
                p = jnp.exp2(s - m_new)
                l_ref[:, cols] = alphas[c] * l_ref[:, cols] + jnp.sum(p, axis=0, keepdims=True)
                m_ref[:, cols] = m_new
                probs[c] = p.astype(MXU_DTYPE)
            for c in active_chunks(rel):
                cols = slice(c * cw, (c + 1) * cw)
                acc_ref[:, cols] = alphas[c] * acc_ref[:, cols] + _dot(vt, probs[c])

        def pair(jj, c):
            scores_into(sb_ref, 2 * jj + 1, None)
            consume(sa_ref, 2 * jj, None)
            scores_into(sa_ref, 2 * jj + 2, None)
            consume(sb_ref, 2 * jj + 1, None)
            return c

        def two_pairs(jq, c):
            pair(2 * jq, c)
            pair(2 * jq + 1, c)
            return c

        scores_into(sa_ref, 0, None)
        lax.fori_loop(0, qi // 2, two_pairs, 0)

        @pl.when(qi % 2 == 1)
        def _():
            pair(qi - 1, 0)

        scores_into(sb_ref, 2 * qi + 1, 1)
        consume(sa_ref, 2 * qi, 0)
        consume(sb_ref, 2 * qi + 1, 1)
        out_t = acc_ref[...] / l_ref[...]
        o_ref[0, pl.ds(q0, tq), :] = out_t.T.astype(o_ref.dtype)
        return carry

    lax.fori_loop(0, seq // tq, q_block, 0)


def _attention(qt, k, vt, tq, cw):
    bsz, nh, seq, _ = k.shape
    tk = vt.shape[4]
    return pl.pallas_call(
        functools.partial(_attn_kernel, tq=tq, tk=tk, cw=cw),
        grid=(bsz, nh),
        in_specs=[pl.BlockSpec((1, 1, seq // cw, QK_DIM, cw), lambda b, h: (b, h, 0, 0, 0)),
                  pl.BlockSpec((1, 1, seq, QK_DIM), lambda b, h: (b, h, 0, 0)),
                  pl.BlockSpec((1, 1, seq // tk, V_HEAD, tk), lambda b, h: (b, h, 0, 0, 0))],
        out_specs=pl.BlockSpec((1, seq, V_HEAD), lambda b, h: (b, 0, h)),
        out_shape=jax.ShapeDtypeStruct((bsz, seq, nh * V_HEAD), MXU_DTYPE),
        scratch_shapes=[pltpu.VMEM((tq // cw, tk, cw), F32),
                        pltpu.VMEM((tq // cw, tk, cw), F32),
                        pltpu.VMEM((1, tq), F32),
                        pltpu.VMEM((1, tq), F32),
                        pltpu.VMEM((V_HEAD, tq), F32)],
        compiler_params=pltpu.CompilerParams(dimension_semantics=("arbitrary", "arbitrary"),
                                             vmem_limit_bytes=VMEM_LIMIT_BYTES),
        name="mla_attention",
    )(qt, k, vt)


def _outproj_kernel(x_ref, pool_ref, att_ref, mod_ref, wout_ref, g_ref, b_ref, x1_ref, h2t_ref):
    mix = _dot(pool_ref[0], wout_ref[0:D_POOL, :]) + _dot(att_ref[0], wout_ref[D_POOL:, :])
    g1 = mod_ref[0, 2:3, :]
    sh2 = mod_ref[0, 3:4, :]
    sc2 = mod_ref[0, 4:5, :]
    x1 = _layer_norm(ALPHA * x_ref[0] + g1 * mix, g_ref[...], b_ref[...])
    x1_ref[0] = x1
    h2t_ref[0] = (x1 * (1.0 + sc2) + sh2).T.astype(h2t_ref.dtype)


def _outproj(x, pool, att, mod, w_out, ln_g, ln_b, tt):
    bsz, seq, d = x.shape
    tile = lambda n: pl.BlockSpec((1, tt, n), lambda b, j: (b, j, 0))
    full = lambda shape: pl.BlockSpec(shape, lambda b, j: (0,) * len(shape))
    return pl.pallas_call(
        _outproj_kernel,
        grid=(bsz, seq // tt),
        in_specs=[tile(d), tile(D_POOL), tile(D_ATT),
                  pl.BlockSpec((1, 6, d), lambda b, j: (b, 0, 0)),
                  full(w_out.shape), full((1, d)), full((1, d))],
        out_specs=[tile(d), pl.BlockSpec((1, d, tt), lambda b, j: (b, 0, j))],
        out_shape=[jax.ShapeDtypeStruct((bsz, seq, d), F32),
                   jax.ShapeDtypeStruct((bsz, d, seq), MXU_DTYPE)],
        compiler_params=pltpu.CompilerParams(dimension_semantics=("arbitrary", "arbitrary"),
                                             vmem_limit_bytes=VMEM_LIMIT_BYTES),
        name="out_proj_ln1",
    )(x, pool, att, mod.reshape(bsz, 6, d), w_out.astype(MXU_DTYPE), ln_g.reshape(1, d),
      ln_b.reshape(1, d))


def _extract_topk(s, k, with_rank=True):
    n = s.shape[0]
    iota = lax.broadcasted_iota(jnp.int32, s.shape, 0).astype(F32)
    rank = jnp.full(s.shape, NOT_SELECTED, F32)
    vals = []
    sub = lax.broadcasted_iota(jnp.int32, (SUBLANES, s.shape[1]), 0).astype(F32)
    for r in range(k):
        nodes = [(s[g:g + SUBLANES], float(g)) for g in range(0, n, SUBLANES)]
        while len(nodes) > 1:
            merged = []
            for a in range(0, len(nodes) - 1, 2):
                (va, ga), (vb, gb) = nodes[a], nodes[a + 1]
                take_b = vb > va
                merged.append((jnp.where(take_b, vb, va), jnp.where(take_b, gb, ga)))
            if len(nodes) % 2:
                merged.append(nodes[-1])
            nodes = merged
        v8, g8 = nodes[0]
        m = jnp.max(v8, axis=0, keepdims=True)
        first = jnp.min(jnp.where(v8 == m, g8 + sub, float(n)), axis=0, keepdims=True)
        sel = iota == first
        if with_rank:
            rank = jnp.where(sel, float(r + 1), rank)
        s = jnp.where(sel, -jnp.inf, s)
        vals.append(m)
    return (rank if with_rank else s == -jnp.inf), jnp.concatenate(vals, axis=0)


def _peer_kernel(h2t_ref, x1_ref, mod_ref, wq_ref, keys_ref, u_ref, vt_ref, g_ref, b_ref, o_ref,
                 qt_ref, lim1_ref, e1_ref, r2_ref, e2_ref, gate_ref, a0_ref, a1_ref, w0_ref, w1_ref,
                 acc_ref):
    a_refs = (a0_ref, a1_ref)
    w_refs = (w0_ref, w1_ref)
    c = pl.program_id(1)
    tt = h2t_ref.shape[2]
    k = PEER_TOPK
    slabs = PEER_NKEYS // SLAB_ROWS
    tiles = PEER_NKEYS // SUBLANES
    n_tb = tt // LANES

    @pl.when(c == 0)
    def _():
        acc_ref[...] = jnp.zeros(acc_ref.shape, F32)
        qt_ref[...] = _dot(wq_ref[...], h2t_ref[0]).astype(qt_ref.dtype)

        def head(hd, carry):
            r0 = pl.multiple_of(hd * 2 * PEER_HALF, 2 * PEER_HALF)
            s1 = _dot(keys_ref[2 * hd], qt_ref[pl.ds(r0, PEER_HALF), :])
            s2 = _dot(keys_ref[2 * hd + 1], qt_ref[pl.ds(r0 + PEER_HALF, PEER_HALF), :])
            rank1, a = _extract_topk(s1, k)
            rank2, b = _extract_topk(s2, k)
            blocks = [a[0:1] + b]
            blocks += [a[r:r + 1] + b[0:8] for r in range(1, 4)]
            blocks += [a[r:r + 1] + b[0:4] for r in range(4, 8)]
            blocks += [a[8:16] + b[0:1]]
            cand = jnp.concatenate(blocks, axis=0)
            sel, _ = _extract_topk(cand, k, with_rank=False)
            sel_f = jnp.where(sel, 1.0, 0.0)
            row_len = [jnp.sum(sel_f[0:16], axis=0, keepdims=True)]
            row_len += [jnp.sum(sel_f[16 + 8 * r:24 + 8 * r], axis=0, keepdims=True) for r in range(3)]
            row_len += [jnp.sum(sel_f[40 + 4 * r:44 + 4 * r], axis=0, keepdims=True) for r in range(4)]
            row_len += [sel_f[56 + r:57 + r] for r in range(8)]
            top = a[0:1] + b[0:1]
            z = jnp.sum(jnp.where(sel, jnp.exp(cand - top), 0.0), axis=0, keepdims=True)
            n_tail = jnp.sum(sel_f[56:64], axis=0, keepdims=True)
            lim1 = jnp.where(rank1 <= 8.0 + n_tail, 1.0, 0.0)
            for r in range(8):
                lim1 = jnp.where(rank1 == float(r + 1), row_len[r], lim1)
            e1 = jnp.exp(s1 - a[0:1]) / z
            r2 = rank2
            e2 = jnp.exp(s2 - b[0:1])
            for tb in range(n_tb):
                ls = slice(tb * LANES, (tb + 1) * LANES)
                lim1_ref[hd, tb] = lim1[:, ls].reshape(tiles, SUBLANES, LANES)
                e1_ref[hd, tb] = e1[:, ls].reshape(tiles, SUBLANES, LANES)
                r2_ref[hd, tb] = r2[:, ls].reshape(slabs, SLAB_ROWS, LANES)
                e2_ref[hd, tb] = e2[:, ls].reshape(slabs, SLAB_ROWS, LANES)
            return carry

        lax.fori_loop(0, PEER_HEADS, head, 0)

    rows = u_ref.shape[0] // PEER_NKEYS
    row_tiles = rows // SUBLANES
    half = SUBLANES // 2

    def gate_block(idx, carry):
        tile = idx // n_tb
        tb = idx % n_tb
        i_tile = c * row_tiles + tile
        for hf in range(2):
            accs = [[None] * slabs for _ in range(half)]
            for hd in range(PEER_HEADS):
                lim_t = lim1_ref[hd, tb, i_tile]
                e1_t = e1_ref[hd, tb, i_tile]
                lims = [jnp.broadcast_to(lim_t[hf * half + ii:hf * half + ii + 1], (SLAB_ROWS, LANES))
                        for ii in range(half)]
                e1rs = [jnp.broadcast_to(e1_t[hf * half + ii:hf * half + ii + 1], (SLAB_ROWS, LANES))
                        for ii in range(half)]
                for sl in range(slabs):
                    r2 = r2_ref[hd, tb, sl]
                    e2 = e2_ref[hd, tb, sl]
                    for ii in range(half):
                        term = jnp.where(r2 <= lims[ii], e2 * e1rs[ii], 0.0)
                        accs[ii][sl] = term if accs[ii][sl] is None else accs[ii][sl] + term
            for ii in range(half):
                for sl in range(slabs):
                    gate_ref[tb, (tile * SUBLANES + hf * half + ii) * slabs + sl] = accs[ii][sl]
        return carry

    lax.fori_loop(0, row_tiles * n_tb, gate_block, 0)

    h2t = h2t_ref[0]
    chain_rows = PEER_ROWS_PER_CHAIN
    chain_e = chain_rows * PEER_NKEYS
    n_chains = rows // chain_rows

    def expert_pre(ch):
        a_refs[ch % 2][...] = _dot(u_ref[ch * chain_e:(ch + 1) * chain_e, :], h2t)

    def gated(ch):
        for tb in range(n_tb):
            ls = slice(tb * LANES, (tb + 1) * LANES)
            for sb in range(chain_rows * slabs):
                r0 = sb * SLAB_ROWS
                a_blk = a_refs[ch % 2][r0:r0 + SLAB_ROWS, ls]
                gelu = 0.5 * a_blk * (1.0 + lax.erf(a_blk * (2.0 ** -0.5)))
                w_blk = gate_ref[tb, ch * chain_rows * slabs + sb] * gelu
                w_refs[ch % 2][r0:r0 + SLAB_ROWS, ls] = w_blk.astype(MXU_DTYPE)

    def expert_post(ch):
        acc_ref[...] += _dot(vt_ref[:, ch * chain_e:(ch + 1) * chain_e], w_refs[ch % 2][...])

    expert_pre(0)
    for ch in range(n_chains):
        if ch > 0:
            expert_post(ch - 1)
        if ch + 1 < n_chains:
            expert_pre(ch + 1)
        gated(ch)
    expert_post(n_chains - 1)

    @pl.when(c == pl.num_programs(1) - 1)
    def _():
        ffn = acc_ref[...].T
        g2 = mod_ref[0, 5:6, :]
        o_ref[...] = _layer_norm(ALPHA * x1_ref[...] + g2 * ffn, g_ref[...], b_ref[...])


def _peer(h2t, x1, mod, w_peer_q, peer_keys, peer_u, peer_v, ln_g, ln_b, seq, tt):
    n_tok, d = x1.shape
    bsz = n_tok // seq
    rows = PEER_ROWS_PER_CHUNK
    ec = rows * PEER_NKEYS
    n_exp = peer_u.shape[0]
    n_q = w_peer_q.shape[1]
    tiles_per_seq = seq // tt
    full = lambda shape: pl.BlockSpec(shape, lambda t, c: (0,) * len(shape))
    assert rows % SUBLANES == 0 and tt % LANES == 0
    n_tb = tt // LANES
    hk = (PEER_HEADS, n_tb, PEER_NKEYS // SUBLANES, SUBLANES, LANES)
    hk_slab = (PEER_HEADS, n_tb, PEER_NKEYS // SLAB_ROWS, SLAB_ROWS, LANES)
    gate_shape = (n_tb, rows * PEER_NKEYS // SLAB_ROWS, SLAB_ROWS, LANES)
    return pl.pallas_call(
        _peer_kernel,
        grid=(n_tok // tt, n_exp // ec),
        in_specs=[pl.BlockSpec((1, d, tt), lambda t, c: (t // tiles_per_seq, 0, t % tiles_per_seq)),
                  pl.BlockSpec((tt, d), lambda t, c: (t, 0)),
                  pl.BlockSpec((1, 6, d), lambda t, c: (t // tiles_per_seq, 0, 0)),
                  full((n_q, d)),
                  full((2 * PEER_HEADS, PEER_NKEYS, PEER_HALF)),
                  pl.BlockSpec((ec, d), lambda t, c: (c, 0)),
                  pl.BlockSpec((d, ec), lambda t, c: (0, c)),
                  full((1, d)), full((1, d))],
        out_specs=pl.BlockSpec((tt, d), lambda t, c: (t, 0)),
        out_shape=jax.ShapeDtypeStruct((n_tok, d), F32),
        scratch_shapes=[pltpu.VMEM((n_q, tt), MXU_DTYPE),
                        pltpu.VMEM(hk, F32), pltpu.VMEM(hk, F32),
                        pltpu.VMEM(hk_slab, F32), pltpu.VMEM(hk_slab, F32),
                        pltpu.VMEM(gate_shape, F32),
                        pltpu.VMEM((PEER_ROWS_PER_CHAIN * PEER_NKEYS, tt), F32),
                        pltpu.VMEM((PEER_ROWS_PER_CHAIN * PEER_NKEYS, tt), F32),
                        pltpu.VMEM((PEER_ROWS_PER_CHAIN * PEER_NKEYS, tt), MXU_DTYPE),
                        pltpu.VMEM((PEER_ROWS_PER_CHAIN * PEER_NKEYS, tt), MXU_DTYPE),
                        pltpu.VMEM((d, tt), F32)],
        compiler_params=pltpu.CompilerParams(dimension_semantics=("arbitrary", "arbitrary"),
                                             vmem_limit_bytes=VMEM_LIMIT_BYTES),
        name="peer_ln2",
    )(h2t, x1, mod.reshape(bsz, 6, d), w_peer_q.T.astype(MXU_DTYPE),
      peer_keys.reshape(2 * PEER_HEADS, PEER_NKEYS, PEER_HALF).astype(MXU_DTYPE),
      peer_u.astype(MXU_DTYPE), peer_v.T.astype(MXU_DTYPE), ln_g.reshape(1, d), ln_b.reshape(1, d))


def kernel(x, c, positions, w_ada, b_ada, w_in, pool_w, pool_scale, q_norm_g, w_uq, kv_norm_g,
           w_ukv, w_out, ln1_g, ln1_b, w_peer_q, peer_keys, peer_u, peer_v, ln2_g, ln2_b):
    bsz, seq, d = x.shape
    assert w_ada.shape[0] == DEPTH
    tt = min(TOKEN_TILE, seq)
    tq = min(ATTN_Q_BLOCK, seq)
    cw = min(ATTN_CHUNK, tt)
    peer_tt = min(PEER_TOKEN_TILE, seq)
    for l in range(DEPTH):
        mod = _adaln(c, w_ada[l], b_ada[l])
        pool, qt, k, vt = _inproj(x, mod, positions, w_in[l], pool_w[l], pool_scale[l],
                                  q_norm_g[l], w_uq[l], kv_norm_g[l], w_ukv[l], tt, cw)
        att = _attention(qt, k, vt, tq, cw)
        x1, h2 = _outproj(x, pool, att, mod, w_out[l], ln1_g[l], ln1_b[l], tt)
        x = _peer(h2, x1.reshape(bsz * seq, d), mod, w_peer_q[l],
                  peer_keys[l], peer_u[l], peer_v[l], ln2_g[l], ln2_b[l], seq,
                  peer_tt).reshape(bsz, seq, d)
    return x
```

```python
import functools

import jax
import jax.numpy as jnp
from jax import lax
from jax.experimental import pallas as pl
from jax.experimental.pallas import tpu as pltpu

POOL_WINDOWS = (2, 4, 8, 16)
POOL_GROUP = 64
D_POOL = POOL_GROUP * len(POOL_WINDOWS)
V_HEAD = 128
QK_NOPE = 128
QK_ROPE = 64
QK_DIM = QK_NOPE + QK_ROPE
N_HEADS = 6
D_ATT = N_HEADS * V_HEAD
Q_LORA = 384
KV_LORA = 256
ROPE_THETA = 10000.0
PEER_HEADS = 8
PEER_NKEYS = 128
PEER_HALF = 128
PEER_TOPK = 16
DEPTH = 1
ALPHA = (2.0 * DEPTH) ** 0.25
LN_EPS = 1e-5
RMS_EPS = 1e-6
LOG2_E = 1.4426950408889634

LANES = 128
SUBLANES = 8
SLAB_ROWS = 16
VMEM_LIMIT_BYTES = 60000 * 1024

MXU_DTYPE = jnp.bfloat16
ROW_TABLE_DTYPE = jnp.bfloat16
F32 = jnp.float32
NOT_SELECTED = 1.0e4

POOL_HALO = 16
TOKEN_TILE = 512
ATTN_Q_BLOCK = 1024
ATTN_CHUNK = 256
PEER_TOKEN_TILE = 512
PEER_ROWS_PER_CHUNK = 16
PEER_ROWS_PER_CHAIN = 8


def _dot(a, b):
    return jnp.dot(a, b, preferred_element_type=F32)


def _dot_nt(a, b):
    return lax.dot_general(a, b, (((1,), (1,)), ((), ())), preferred_element_type=F32)


def _layer_norm(y, g, b):
    mu = jnp.mean(y, axis=-1, keepdims=True)
    d = y - mu
    var = jnp.mean(d * d, axis=-1, keepdims=True)
    return d * lax.rsqrt(var + LN_EPS) * g + b


def _rms_norm(x, g):
    ms = jnp.mean(x * x, axis=-1, keepdims=True)
    return x * lax.rsqrt(ms + RMS_EPS) * g


def _adaln_kernel(c_ref, w_ref, b_ref, o_ref):
    c = c_ref[...]
    c_act = c * jax.nn.sigmoid(c)
    o_ref[...] = _dot(c_act.astype(MXU_DTYPE), w_ref[...].astype(MXU_DTYPE)) + b_ref[...]


def _adaln(c, w_ada, b_ada):
    bsz, d = c.shape
    n = w_ada.shape[1]
    blk = d
    return pl.pallas_call(
        _adaln_kernel,
        grid=(n // blk,),
        in_specs=[pl.BlockSpec((bsz, d), lambda i: (0, 0)),
                  pl.BlockSpec((d, blk), lambda i: (0, i)),
                  pl.BlockSpec((1, blk), lambda i: (0, i))],
        out_specs=pl.BlockSpec((bsz, blk), lambda i: (0, i)),
        out_shape=jax.ShapeDtypeStruct((bsz, n), F32),
        compiler_params=pltpu.CompilerParams(dimension_semantics=("arbitrary",),
                                             vmem_limit_bytes=VMEM_LIMIT_BYTES),
        name="adaln_mod",
    )(c, w_ada, b_ada.reshape(1, n))


def _inproj_kernel(x_ref, mod_ref, pos_ref, invf_ref, win_ref, wpool_ref, pscale_ref, qg_ref,
                   wuqt_ref, kvg_ref, wuk_ref, wuvt_ref, pool_ref, qt_ref, k_ref, vt_ref, ext_ref):
    j = pl.program_id(1)
    tt = x_ref.shape[1]
    x = x_ref[0]
    sh1 = mod_ref[0, 0:1, :]
    sc1 = mod_ref[0, 1:2, :]
    h = x * (1.0 + sc1) + sh1
    z = _dot(h.astype(MXU_DTYPE), win_ref[...])

    @pl.when(j == 0)
    def _():
        ext_ref[0:POOL_HALO, :] = jnp.zeros((POOL_HALO, D_POOL), F32)

    p = z[:, :D_POOL]
    ext_ref[POOL_HALO:POOL_HALO + tt, :] = p

    def shifted(d, col):
        return ext_ref[pl.ds(POOL_HALO - d, tt), col * LANES:(col + 1) * LANES]

    lane = lax.broadcasted_iota(jnp.int32, (tt, LANES), 1)
    low = lane < POOL_GROUP
    t_seq = (j * tt + lax.broadcasted_iota(jnp.int32, (tt, LANES), 0) + 1).astype(F32)
    mixed = []
    for col in range(2):
        w_lo, w_hi = POOL_WINDOWS[2 * col], POOL_WINDOWS[2 * col + 1]
        s_lo = shifted(0, col)
        for d in range(1, w_lo):
            s_lo = s_lo + shifted(d, col)
        s_hi = s_lo
        for d in range(w_lo, w_hi):
            s_hi = s_hi + shifted(d, col)
        win_sum = jnp.where(low, s_lo, s_hi)
        count = jnp.minimum(t_seq, jnp.where(low, float(w_lo), float(w_hi)))
        mixed.append(win_sum / count - p[:, col * LANES:(col + 1) * LANES])
    mixed = jnp.concatenate(mixed, axis=1)
    ext_ref[0:POOL_HALO, :] = ext_ref[tt:tt + POOL_HALO, :]
    pool = _dot(mixed.astype(MXU_DTYPE), wpool_ref[...]) * pscale_ref[...]
    pool_ref[0] = pool.astype(pool_ref.dtype)

    ang_t = invf_ref[...] * pos_ref[0].astype(F32)
    cos_t = jnp.cos(ang_t)
    sin_t = jnp.sin(ang_t)
    cos_t = jnp.concatenate([cos_t, cos_t], axis=0)
    sin_t = jnp.concatenate([sin_t, sin_t], axis=0)

    o1 = D_POOL
    o2 = o1 + Q_LORA
    o3 = o2 + KV_LORA
    cqn = _rms_norm(z[:, o1:o2], qg_ref[...]).astype(MXU_DTYPE)
    qq_t = _dot_nt(wuqt_ref[...], cqn)
    n_nope = N_HEADS * QK_NOPE
    n_rope = N_HEADS * QK_ROPE
    scale = QK_DIM ** -0.5 * LOG2_E
    cw = qt_ref.shape[4]
    for hd in range(N_HEADS):
        r0 = n_nope + hd * QK_ROPE
        rope_t = qq_t[r0:r0 + QK_ROPE] * cos_t + qq_t[r0 + n_rope:r0 + n_rope + QK_ROPE] * sin_t
        qh_t = (jnp.concatenate([qq_t[hd * QK_NOPE:(hd + 1) * QK_NOPE], rope_t], axis=0)
                * scale).astype(qt_ref.dtype)
        for cc in range(tt // cw):
            qt_ref[0, hd, cc] = qh_t[:, cc * cw:(cc + 1) * cw]

    ckvn = _rms_norm(z[:, o2:o3], kvg_ref[...]).astype(MXU_DTYPE)
    k_nope = _dot(ckvn, wuk_ref[...])
    v_t = _dot_nt(wuvt_ref[...], ckvn)
    cs = jnp.concatenate([cos_t, sin_t], axis=0).T
    kk = z[:, o3:] * cs
    k_rope = (kk + pltpu.roll(kk, QK_ROPE, 1))[:, :QK_ROPE]
    for hd in range(N_HEADS):
        kh = jnp.concatenate([k_nope[:, hd * QK_NOPE:(hd + 1) * QK_NOPE], k_rope], axis=1)
        k_ref[0, hd] = kh.astype(k_ref.dtype)
        vt_ref[0, hd, 0] = v_t[hd * V_HEAD:(hd + 1) * V_HEAD].astype(vt_ref.dtype)


def _rotate_half_cols(w):
    half = w.shape[-1] // 2
    return jnp.concatenate([-w[..., half:], w[..., :half]], axis=-1)


def _inproj(x, mod, positions, w_in, pool_w, pool_scale, q_norm_g, w_uq, kv_norm_g, w_ukv, tt, cw):
    bsz, seq, d = x.shape
    o3 = D_POOL + Q_LORA + KV_LORA
    w_in_x = jnp.concatenate([w_in, _rotate_half_cols(w_in[:, o3:])], axis=1).astype(MXU_DTYPE)
    n_groups = len(POOL_WINDOWS)
    w_pool_bd = jnp.zeros((D_POOL, D_POOL), F32)
    for g in range(n_groups):
        w_pool_bd = w_pool_bd.at[g * POOL_GROUP:(g + 1) * POOL_GROUP,
                                 g * POOL_GROUP:(g + 1) * POOL_GROUP].set(pool_w[g])
    w_pool_bd = w_pool_bd.astype(MXU_DTYPE)
    w_uq3 = w_uq.reshape(Q_LORA, N_HEADS, QK_DIM)
    w_q_rope = w_uq3[:, :, QK_NOPE:]
    w_uq_t = jnp.concatenate([w_uq3[:, :, :QK_NOPE].reshape(Q_LORA, N_HEADS * QK_NOPE),
                              w_q_rope.reshape(Q_LORA, N_HEADS * QK_ROPE),
                              _rotate_half_cols(w_q_rope).reshape(Q_LORA, N_HEADS * QK_ROPE)],
                             axis=1).T.astype(MXU_DTYPE)
    w_ukv3 = w_ukv.reshape(KV_LORA, N_HEADS, QK_NOPE + V_HEAD)
    w_uk = w_ukv3[:, :, :QK_NOPE].reshape(KV_LORA, N_HEADS * QK_NOPE).astype(MXU_DTYPE)
    w_uv_t = w_ukv3[:, :, QK_NOPE:].reshape(KV_LORA, N_HEADS * V_HEAD).T.astype(MXU_DTYPE)
    inv_freq = ROPE_THETA ** (-jnp.arange(0, QK_ROPE, 2, dtype=F32) / QK_ROPE)
    invf = inv_freq.reshape(QK_ROPE // 2, 1)
    full = lambda shape: pl.BlockSpec(shape, lambda b, j: (0,) * len(shape))
    return pl.pallas_call(
        _inproj_kernel,
        grid=(bsz, seq // tt),
        in_specs=[pl.BlockSpec((1, tt, d), lambda b, j: (b, j, 0)),
                  pl.BlockSpec((1, 6, d), lambda b, j: (b, 0, 0)),
                  pl.BlockSpec((1, 1, tt), lambda b, j: (b, 0, j)),
                  full(invf.shape),
                  full(w_in_x.shape),
                  full((D_POOL, D_POOL)),
                  full((1, D_POOL)),
                  full((1, Q_LORA)),
                  full(w_uq_t.shape),
                  full((1, KV_LORA)),
                  full(w_uk.shape),
                  full(w_uv_t.shape)],
        out_specs=[pl.BlockSpec((1, tt, D_POOL), lambda b, j: (b, j, 0)),
                   pl.BlockSpec((1, N_HEADS, tt // cw, QK_DIM, cw), lambda b, j: (b, 0, j, 0, 0)),
                   pl.BlockSpec((1, N_HEADS, tt, QK_DIM), lambda b, j: (b, 0, j, 0)),
                   pl.BlockSpec((1, N_HEADS, 1, V_HEAD, tt), lambda b, j: (b, 0, j, 0, 0))],
        out_shape=[jax.ShapeDtypeStruct((bsz, seq, D_POOL), MXU_DTYPE),
                   jax.ShapeDtypeStruct((bsz, N_HEADS, seq // cw, QK_DIM, cw), MXU_DTYPE),
                   jax.ShapeDtypeStruct((bsz, N_HEADS, seq, QK_DIM), MXU_DTYPE),
                   jax.ShapeDtypeStruct((bsz, N_HEADS, seq // tt, V_HEAD, tt), MXU_DTYPE)],
        scratch_shapes=[pltpu.VMEM((tt + POOL_HALO, D_POOL), F32)],
        compiler_params=pltpu.CompilerParams(dimension_semantics=("arbitrary", "arbitrary"),
                                             vmem_limit_bytes=VMEM_LIMIT_BYTES),
        name="in_proj",
    )(x, mod.reshape(bsz, 6, d), positions.reshape(bsz, 1, seq), invf, w_in_x, w_pool_bd,
      pool_scale.reshape(1, D_POOL), q_norm_g.reshape(1, Q_LORA), w_uq_t,
      kv_norm_g.reshape(1, KV_LORA), w_uk, w_uv_t)


def _attn_kernel(qt_ref, k_ref, vt_ref, o_ref, sa_ref, sb_ref, m_ref, l_ref, acc_ref, *, tq, tk, cw):
    seq = k_ref.shape[2]
    n_chunks = tq // cw
    assert tq == 2 * tk

    def q_block(qi, carry):
        q0 = pl.multiple_of(qi * tq, tq)
        m_ref[...] = jnp.full(m_ref.shape, -jnp.inf, F32)
        l_ref[...] = jnp.zeros(l_ref.shape, F32)
        acc_ref[...] = jnp.zeros(acc_ref.shape, F32)

        def active_chunks(rel):
            return [c for c in range(n_chunks) if rel is None or (c + 1) * cw > rel * tk]

        def scores_into(buf_ref, kj, rel):
            k0 = pl.multiple_of(kj * tk, tk)
            k = k_ref[0, 0, pl.ds(k0, tk), :]
            for c in active_chunks(rel):
                buf_ref[c] = _dot(k, qt_ref[0, 0, qi * n_chunks + c])

        def consume(buf_ref, kj, rel):
            vt = vt_ref[0, 0, kj]
            probs, alphas = {}, {}
            for c in active_chunks(rel):
                cols = slice(c * cw, (c + 1) * cw)
                s = buf_ref[c]
                if rel is not None and c * cw < (rel + 1) * tk:
                    kv_idx = rel * tk + lax.broadcasted_iota(jnp.int32, (tk, cw), 0)
                    q_idx = c * cw + lax.broadcasted_iota(jnp.int32, (tk, cw), 1)
                    s = jnp.where(q_idx >= kv_idx, s, -jnp.inf)
                m_prev = m_ref[:, cols]
                m_new = jnp.maximum(m_prev, jnp.max(s, axis=0, keepdims=True))
                alphas[c] = jnp.exp2(m_prev - m_new)
                p = jnp.exp2(s - m_new)
                l_ref[:, cols] = alphas[c] * l_ref[:, cols] + jnp.sum(p, axis=0, keepdims=True)
                m_ref[:, cols] = m_new
                probs[c] = p.astype(MXU_DTYPE)
            for c in active_chunks(rel):
                cols = slice(c * cw, (c + 1) * cw)
                acc_ref[:, cols] = alphas[c] * acc_ref[:, cols] + _dot(vt, probs[c])

        def pair(jj, c):
            scores_into(sb_ref, 2 * jj + 1, None)
            consume(sa_ref, 2 * jj, None)
            scores_into(sa_ref, 2 * jj + 2, None)
            consume(sb_ref, 2 * jj + 1, None)
            return c

        def two_pairs(jq, c):
            pair(2 * jq, c)
            pair(2 * jq + 1, c)
            return c

        scores_into(sa_ref, 0, None)
        lax.fori_loop(0, qi // 2, two_pairs, 0)

        @pl.when(qi % 2 == 1)
        def _():
            pair(qi - 1, 0)

        scores_into(sb_ref, 2 * qi + 1, 1)
        consume(sa_ref, 2 * qi, 0)
        consume(sb_ref, 2 * qi + 1, 1)
        out_t = acc_ref[...] / l_ref[...]
        o_ref[0, pl.ds(q0, tq), :] = out_t.T.astype(o_ref.dtype)
        return carry

    lax.fori_loop(0, seq // tq, q_block, 0)


def _attention(qt, k, vt, tq, cw):
    bsz, nh, seq, _ = k.shape
    tk = vt.shape[4]
    return pl.pallas_call(
        functools.partial(_attn_kernel, tq=tq, tk=tk, cw=cw),
        grid=(bsz, nh),
        in_specs=[pl.BlockSpec((1, 1, seq // cw, QK_DIM, cw), lambda b, h: (b, h, 0, 0, 0)),
                  pl.BlockSpec((1, 1, seq, QK_DIM), lambda b, h: (b, h, 0, 0)),
                  pl.BlockSpec((1, 1, seq // tk, V_HEAD, tk), lambda b, h: (b, h, 0, 0, 0))],
        out_specs=pl.BlockSpec((1, seq, V_HEAD), lambda b, h: (b, 0, h)),
        out_shape=jax.ShapeDtypeStruct((bsz, seq, nh * V_HEAD), MXU_DTYPE),
        scratch_shapes=[pltpu.VMEM((tq // cw, tk, cw), F32),
                        pltpu.VMEM((tq // cw, tk, cw), F32),
                        pltpu.VMEM((1, tq), F32),
                        pltpu.VMEM((1, tq), F32),
                        pltpu.VMEM((V_HEAD, tq), F32)],
        compiler_params=pltpu.CompilerParams(dimension_semantics=("arbitrary", "arbitrary"),
                                             vmem_limit_bytes=VMEM_LIMIT_BYTES),
        name="mla_attention",
    )(qt, k, vt)


def _outproj_kernel(x_ref, pool_ref, att_ref, mod_ref, wout_ref, g_ref, b_ref, x1_ref, h2t_ref):
    mix = _dot(pool_ref[0], wout_ref[0:D_POOL, :]) + _dot(att_ref[0], wout_ref[D_POOL:, :])
    g1 = mod_ref[0, 2:3, :]
    sh2 = mod_ref[0, 3:4, :]
    sc2 = mod_ref[0, 4:5, :]
    x1 = _layer_norm(ALPHA * x_ref[0] + g1 * mix, g_ref[...], b_ref[...])
    x1_ref[0] = x1
    h2t_ref[0] = (x1 * (1.0 + sc2) + sh2).T.astype(h2t_ref.dtype)


def _outproj(x, pool, att, mod, w_out, ln_g, ln_b, tt):
    bsz, seq, d = x.shape
    tile = lambda n: pl.BlockSpec((1, tt, n), lambda b, j: (b, j, 0))
    full = lambda shape: pl.BlockSpec(shape, lambda b, j: (0,) * len(shape))
    return pl.pallas_call(
        _outproj_kernel,
        grid=(bsz, seq // tt),
        in_specs=[tile(d), tile(D_POOL), tile(D_ATT),
                  pl.BlockSpec((1, 6, d), lambda b, j: (b, 0, 0)),
                  full(w_out.shape), full((1, d)), full((1, d))],
        out_specs=[tile(d), pl.BlockSpec((1, d, tt), lambda b, j: (b, 0, j))],
        out_shape=[jax.ShapeDtypeStruct((bsz, seq, d), F32),
                   jax.ShapeDtypeStruct((bsz, d, seq), MXU_DTYPE)],
        compiler_params=pltpu.CompilerParams(dimension_semantics=("arbitrary", "arbitrary"),
                                             vmem_limit_bytes=VMEM_LIMIT_BYTES),
        name="out_proj_ln1",
    )(x, pool, att, mod.reshape(bsz, 6, d), w_out.astype(MXU_DTYPE), ln_g.reshape(1, d),
      ln_b.reshape(1, d))


def _extract_topk(s, k, fence=None, with_rank=True):
    n = s.shape[0]
    iota = lax.broadcasted_iota(jnp.int32, s.shape, 0).astype(F32)
    rank = jnp.full(s.shape, NOT_SELECTED, F32)
    vals = []
    sub = lax.broadcasted_iota(jnp.int32, (SUBLANES, s.shape[1]), 0).astype(F32)
    for r in range(k):
        zeros = fence() if fence is not None else None
        if zeros is not None:
            s = s + jnp.concatenate([zeros] * (n // SUBLANES), axis=0)
        nodes = [(s[g:g + SUBLANES], float(g)) for g in range(0, n, SUBLANES)]
        while len(nodes) > 1:
            merged = []
            for a in range(0, len(nodes) - 1, 2):
                (va, ga), (vb, gb) = nodes[a], nodes[a + 1]
                take_b = vb > va
                merged.append((jnp.where(take_b, vb, va), jnp.where(take_b, gb, ga)))
            if len(nodes) % 2:
                merged.append(nodes[-1])
            nodes = merged
        v8, g8 = nodes[0]
        m = jnp.max(v8, axis=0, keepdims=True)
        first = jnp.min(jnp.where(v8 == m, g8 + sub, float(n)), axis=0, keepdims=True)
        sel = iota == first
        if with_rank:
            rank = jnp.where(sel, float(r + 1), rank)
        s = jnp.where(sel, -jnp.inf, s)
        vals.append(m)
        yield
    return (rank if with_rank else s == -jnp.inf), jnp.concatenate(vals, axis=0)


def _peer_kernel(h2t_ref, h2t_next_ref, x1_ref, mod_ref, wq_ref, keys_ref, u_ref, vt_ref, g_ref, b_ref,
                 o_ref, lim1_ref, e1_ref, r2_ref, e2_ref, gate_ref, sc_ref, a0_ref, a1_ref, w0_ref, w1_ref,
                 acc_ref):
    a_refs = (a0_ref, a1_ref)
    w_refs = (w0_ref, w1_ref)
    t = pl.program_id(0)
    c = pl.program_id(1)
    tt = h2t_ref.shape[2]
    k = PEER_TOPK
    slabs = PEER_NKEYS // SLAB_ROWS
    n_tb = tt // LANES
    slot = t % 2

    def head_scores(hd, h2t):
        r0 = pl.multiple_of(hd * 2 * PEER_HALF, 2 * PEER_HALF)
        q_t = _dot(wq_ref[pl.ds(r0, 2 * PEER_HALF), :], h2t).astype(MXU_DTYPE)
        return jnp.stack([_dot(keys_ref[2 * hd], q_t[:PEER_HALF]),
                          _dot(keys_ref[2 * hd + 1], q_t[PEER_HALF:])])

    def head_tables(scores, hd, dst, fence=None):
        s1, s2 = scores[0], scores[1]
        rank1, a = yield from _extract_topk(s1, k, fence)
        rank2, b = yield from _extract_topk(s2, k, fence)
        blocks = [a[0:1] + b]
        blocks += [a[r:r + 1] + b[0:8] for r in range(1, 4)]
        blocks += [a[r:r + 1] + b[0:4] for r in range(4, 8)]
        blocks += [a[8:16] + b[0:1]]
        cand = jnp.concatenate(blocks, axis=0)
        sel, _ = yield from _extract_topk(cand, k, fence, with_rank=False)
        sel_f = jnp.where(sel, 1.0, 0.0)
        row_len = [jnp.sum(sel_f[0:16], axis=0, keepdims=True)]
        row_len += [jnp.sum(sel_f[16 + 8 * r:24 + 8 * r], axis=0, keepdims=True) for r in range(3)]
        row_len += [jnp.sum(sel_f[40 + 4 * r:44 + 4 * r], axis=0, keepdims=True) for r in range(4)]
        row_len += [sel_f[56 + r:57 + r] for r in range(8)]
        top = a[0:1] + b[0:1]
        z = jnp.sum(jnp.where(sel, jnp.exp(cand - top), 0.0), axis=0, keepdims=True)
        yield
        n_tail = jnp.sum(sel_f[56:64], axis=0, keepdims=True)
        lim1 = jnp.where(rank1 <= 8.0 + n_tail, 1.0, 0.0)
        for r in range(8):
            lim1 = jnp.where(rank1 == float(r + 1), row_len[r], lim1)
            if r % 4 == 3:
                yield
        e1 = jnp.exp(s1 - a[0:1]) / z
        e2 = jnp.exp(s2 - b[0:1])
        for tb in range(n_tb):
            ls = slice(tb * LANES, (tb + 1) * LANES)
            lim1_ref[dst, hd, tb] = lim1[:, ls].astype(ROW_TABLE_DTYPE).reshape(slabs, SLAB_ROWS, LANES)
            e1_ref[dst, hd, tb] = e1[:, ls].astype(ROW_TABLE_DTYPE).reshape(slabs, SLAB_ROWS, LANES)
            r2_ref[dst, hd, tb] = rank2[:, ls].reshape(slabs, SLAB_ROWS, LANES)
            e2_ref[dst, hd, tb] = e2[:, ls].reshape(slabs, SLAB_ROWS, LANES)

    @pl.when(c == 0)
    def _():
        acc_ref[...] = jnp.zeros(acc_ref.shape, F32)
        sc_ref[...] = head_scores(0, h2t_next_ref[0])

    @pl.when((c == 0) & (t == 0))
    def _():
        def head(hd, carry):
            for _ in head_tables(head_scores(hd, h2t_ref[0]), hd, 0):
                pass
            return carry

        lax.fori_loop(0, PEER_HEADS, head, 0)

    rows = u_ref.shape[0] // PEER_NKEYS
    group = 4

    def gate_block(tb, carry):
        for row0 in range(0, rows, group):
            accs = [[None] * slabs for _ in range(group)]
            for hd in range(PEER_HEADS):
                lim_t = lim1_ref[slot, hd, tb, c].astype(F32)
                e1_t = e1_ref[slot, hd, tb, c].astype(F32)
                lims = [jnp.broadcast_to(lim_t[row0 + ii:row0 + ii + 1], (SLAB_ROWS, LANES))
                        for ii in range(group)]
                e1rs = [jnp.broadcast_to(e1_t[row0 + ii:row0 + ii + 1], (SLAB_ROWS, LANES))
                        for ii in range(group)]
                for sl in range(slabs):
                    r2 = r2_ref[slot, hd, tb, sl]
                    e2 = e2_ref[slot, hd, tb, sl]
                    for ii in range(group):
                        term = jnp.where(r2 <= lims[ii], e2 * e1rs[ii], 0.0)
                        accs[ii][sl] = term if accs[ii][sl] is None else accs[ii][sl] + term
            for ii in range(group):
                for sl in range(slabs):
                    gate_ref[tb, (row0 + ii) * slabs + sl] = accs[ii][sl]
        return carry

    lax.fori_loop(0, n_tb, gate_block, 0)

    h2t = h2t_ref[0]
    chain_rows = PEER_ROWS_PER_CHAIN
    chain_e = chain_rows * PEER_NKEYS
    n_chains = rows // chain_rows
    fences = []
    retrieval = head_tables(sc_ref[...], c, 1 - slot, lambda: fences.pop() if fences else None)
    n_pieces = 3 * k + 3
    per_slice = -(-n_pieces // (n_chains + 1))

    def retrieval_slice():
        for _ in range(per_slice):
            next(retrieval, None)

    def expert_pre(ch):
        a_refs[ch % 2][...] = _dot(u_ref[ch * chain_e:(ch + 1) * chain_e, :], h2t)

    def gated(ch):
        last = []
        for tb in range(n_tb):
            ls = slice(tb * LANES, (tb + 1) * LANES)
            for sb in range(chain_rows * slabs):
                r0 = sb * SLAB_ROWS
                a_blk = a_refs[ch % 2][r0:r0 + SLAB_ROWS, ls]
                gelu = 0.5 * a_blk * (1.0 + lax.erf(a_blk * (2.0 ** -0.5)))
                w_blk = gate_ref[tb, ch * chain_rows * slabs + sb] * gelu
                w_refs[ch % 2][r0:r0 + SLAB_ROWS, ls] = w_blk.astype(MXU_DTYPE)
            last.append(w_blk[:SUBLANES])
        bits = lax.bitcast_convert_type(jnp.concatenate(last, axis=1), jnp.uint32)
        bits = lax.shift_right_logical(lax.shift_right_logical(bits, jnp.uint32(16)), jnp.uint32(16))
        fences.append(lax.bitcast_convert_type(bits, F32))

    def expert_post(ch):
        acc_ref[...] += _dot(vt_ref[:, ch * chain_e:(ch + 1) * chain_e], w_refs[ch % 2][...])

    expert_pre(0)
    retrieval_slice()
    for ch in range(n_chains):
        if ch > 0:
            expert_post(ch - 1)
        if ch + 1 < n_chains:
            expert_pre(ch + 1)
        gated(ch)
        retrieval_slice()
    expert_post(n_chains - 1)
    for _ in retrieval:
        pass
    sc_ref[...] = head_scores(jnp.minimum(c + 1, PEER_HEADS - 1), h2t_next_ref[0])

    @pl.when(c == pl.num_programs(1) - 1)
    def _():
        ffn = acc_ref[...].T
        g2 = mod_ref[0, 5:6, :]
        o_ref[...] = _layer_norm(ALPHA * x1_ref[...] + g2 * ffn, g_ref[...], b_ref[...])


def _peer(h2t, x1, mod, w_peer_q, peer_keys, peer_u, peer_v, ln_g, ln_b, seq, tt):
    n_tok, d = x1.shape
    bsz = n_tok // seq
    rows = PEER_ROWS_PER_CHUNK
    ec = rows * PEER_NKEYS
    n_exp = peer_u.shape[0]
    n_q = w_peer_q.shape[1]
    tiles_per_seq = seq // tt
    n_tiles = n_tok // tt
    assert n_exp // ec == PEER_HEADS and rows == SLAB_ROWS and tt % LANES == 0
    n_tb = tt // LANES
    hk_slab = (2, PEER_HEADS, n_tb, PEER_NKEYS // SLAB_ROWS, SLAB_ROWS, LANES)
    gate_shape = (n_tb, rows * PEER_NKEYS // SLAB_ROWS, SLAB_ROWS, LANES)
    full = lambda shape: pl.BlockSpec(shape, lambda t, c: (0,) * len(shape),
                                      pipeline_mode=pl.Buffered(1))

    def h2t_spec(offset):
        def index(t, c):
            tn = jnp.minimum(t + offset, n_tiles - 1)
            return (tn // tiles_per_seq, 0, tn % tiles_per_seq)
        return pl.BlockSpec((1, d, tt), index)

    return pl.pallas_call(
        _peer_kernel,
        grid=(n_tiles, n_exp // ec),
        in_specs=[h2t_spec(0), h2t_spec(1),
                  pl.BlockSpec((tt, d), lambda t, c: (t, 0)),
                  pl.BlockSpec((1, 6, d), lambda t, c: (t // tiles_per_seq, 0, 0)),
                  full((n_q, d)),
                  full((2 * PEER_HEADS, PEER_NKEYS, PEER_HALF)),
                  pl.BlockSpec((ec, d), lambda t, c: (c, 0)),
                  pl.BlockSpec((d, ec), lambda t, c: (0, c)),
                  full((1, d)), full((1, d))],
        out_specs=pl.BlockSpec((tt, d), lambda t, c: (t, 0)),
        out_shape=jax.ShapeDtypeStruct((n_tok, d), F32),
        scratch_shapes=[pltpu.VMEM(hk_slab, ROW_TABLE_DTYPE), pltpu.VMEM(hk_slab, ROW_TABLE_DTYPE),
                        pltpu.VMEM(hk_slab, F32), pltpu.VMEM(hk_slab, F32),
                        pltpu.VMEM(gate_shape, F32),
                        pltpu.VMEM((2, PEER_NKEYS, tt), F32),
                        pltpu.VMEM((PEER_ROWS_PER_CHAIN * PEER_NKEYS, tt), F32),
                        pltpu.VMEM((PEER_ROWS_PER_CHAIN * PEER_NKEYS, tt), F32),
                        pltpu.VMEM((PEER_ROWS_PER_CHAIN * PEER_NKEYS, tt), MXU_DTYPE),
                        pltpu.VMEM((PEER_ROWS_PER_CHAIN * PEER_NKEYS, tt), MXU_DTYPE),
                        pltpu.VMEM((d, tt), F32)],
        compiler_params=pltpu.CompilerParams(dimension_semantics=("arbitrary", "arbitrary"),
                                             vmem_limit_bytes=VMEM_LIMIT_BYTES),
        name="peer_ln2",
    )(h2t, h2t, x1, mod.reshape(bsz, 6, d), w_peer_q.T.astype(MXU_DTYPE),
      peer_keys.reshape(2 * PEER_HEADS, PEER_NKEYS, PEER_HALF).astype(MXU_DTYPE),
      peer_u.astype(MXU_DTYPE), peer_v.T.astype(MXU_DTYPE), ln_g.reshape(1, d), ln_b.reshape(1, d))


def kernel(x, c, positions, w_ada, b_ada, w_in, pool_w, pool_scale, q_norm_g, w_uq, kv_norm_g,
           w_ukv, w_out, ln1_g, ln1_b, w_peer_q, peer_keys, peer_u, peer_v, ln2_g, ln2_b):
    bsz, seq, d = x.shape
    assert w_ada.shape[0] == DEPTH
    tt = min(TOKEN_TILE, seq)
    tq = min(ATTN_Q_BLOCK, seq)
    cw = min(ATTN_CHUNK, tt)
    peer_tt = min(PEER_TOKEN_TILE, seq)
    for l in range(DEPTH):
        mod = _adaln(c, w_ada[l], b_ada[l])
        pool, qt, k, vt = _inproj(x, mod, positions, w_in[l], pool_w[l], pool_scale[l],
                                  q_norm_g[l], w_uq[l], kv_norm_g[l], w_ukv[l], tt, cw)
        att = _attention(qt, k, vt, tq, cw)
        x1, h2 = _outproj(x, pool, att, mod, w_out[l], ln1_g[l], ln1_b[l], tt)
        x = _peer(h2, x1.reshape(bsz * seq, d), mod, w_peer_q[l],
                  peer_keys[l], peer_u[l], peer_v[l], ln2_g[l], ln2_b[l], seq,
                  peer_tt).reshape(bsz, seq, d)
    return x
```

```python
import functools

import jax
import jax.numpy as jnp
from jax import lax
from jax.experimental import pallas as pl
from jax.experimental.pallas import tpu as pltpu

POOL_WINDOWS = (2, 4, 8, 16)
POOL_GROUP = 64
D_POOL = POOL_GROUP * len(POOL_WINDOWS)
V_HEAD = 128
QK_NOPE = 128
QK_ROPE = 64
QK_DIM = QK_NOPE + QK_ROPE
N_HEADS = 6
D_ATT = N_HEADS * V_HEAD
Q_LORA = 384
KV_LORA = 256
ROPE_THETA = 10000.0
PEER_HEADS = 8
PEER_NKEYS = 128
PEER_HALF = 128
PEER_TOPK = 16
DEPTH = 1
ALPHA = (2.0 * DEPTH) ** 0.25
LN_EPS = 1e-5
RMS_EPS = 1e-6
LOG2_E = 1.4426950408889634

LANES = 128
SUBLANES = 8
SLAB_ROWS = 16
VMEM_LIMIT_BYTES = 60000 * 1024

MXU_DTYPE = jnp.bfloat16
ROW_TABLE_DTYPE = jnp.bfloat16
F32 = jnp.float32
NOT_SELECTED = 1.0e4

POOL_HALO = 16
TOKEN_TILE = 512
ATTN_Q_BLOCK = 1024
ATTN_CHUNK = 256
PEER_TOKEN_TILE = 512
PEER_ROWS_PER_CHUNK = 16
PEER_ROWS_PER_CHAIN = 8


def _dot(a, b):
    return jnp.dot(a, b, preferred_element_type=F32)


def _dot_nt(a, b):
    return lax.dot_general(a, b, (((1,), (1,)), ((), ())), preferred_element_type=F32)


def _layer_norm(y, g, b):
    mu = jnp.mean(y, axis=-1, keepdims=True)
    d = y - mu
    var = jnp.mean(d * d, axis=-1, keepdims=True)
    return d * lax.rsqrt(var + LN_EPS) * g + b


def _rms_norm(x, g):
    ms = jnp.mean(x * x, axis=-1, keepdims=True)
    return x * lax.rsqrt(ms + RMS_EPS) * g


def _adaln_kernel(c_ref, w_ref, b_ref, o_ref):
    c = c_ref[...]
    c_act = c * jax.nn.sigmoid(c)
    o_ref[...] = _dot(c_act.astype(MXU_DTYPE), w_ref[...].astype(MXU_DTYPE)) + b_ref[...]


def _adaln(c, w_ada, b_ada):
    bsz, d = c.shape
    n = w_ada.shape[1]
    blk = d
    return pl.pallas_call(
        _adaln_kernel,
        grid=(n // blk,),
        in_specs=[pl.BlockSpec((bsz, d), lambda i: (0, 0)),
                  pl.BlockSpec((d, blk), lambda i: (0, i)),
                  pl.BlockSpec((1, blk), lambda i: (0, i))],
        out_specs=pl.BlockSpec((bsz, blk), lambda i: (0, i)),
        out_shape=jax.ShapeDtypeStruct((bsz, n), F32),
        compiler_params=pltpu.CompilerParams(dimension_semantics=("arbitrary",),
                                             vmem_limit_bytes=VMEM_LIMIT_BYTES),
        name="adaln_mod",
    )(c, w_ada, b_ada.reshape(1, n))


def _inproj_kernel(x_ref, mod_ref, pos_ref, invf_ref, win_ref, wpool_ref, pscale_ref, qg_ref,
                   wuqt_ref, kvg_ref, wuk_ref, wuvt_ref, pool_ref, qt_ref, k_ref, vt_ref, ext_ref):
    j = pl.program_id(1)
    tt = x_ref.shape[1]
    x = x_ref[0]
    sh1 = mod_ref[0, 0:1, :]
    sc1 = mod_ref[0, 1:2, :]
    h = x * (1.0 + sc1) + sh1
    z = _dot(h.astype(MXU_DTYPE), win_ref[...])

    @pl.when(j == 0)
    def _():
        ext_ref[0:POOL_HALO, :] = jnp.zeros((POOL_HALO, D_POOL), F32)

    p = z[:, :D_POOL]
    ext_ref[POOL_HALO:POOL_HALO + tt, :] = p

    def shifted(d, col):
        return ext_ref[pl.ds(POOL_HALO - d, tt), col * LANES:(col + 1) * LANES]

    lane = lax.broadcasted_iota(jnp.int32, (tt, LANES), 1)
    low = lane < POOL_GROUP
    t_seq = (j * tt + lax.broadcasted_iota(jnp.int32, (tt, LANES), 0) + 1).astype(F32)
    mixed = []
    for col in range(2):
        w_lo, w_hi = POOL_WINDOWS[2 * col], POOL_WINDOWS[2 * col + 1]
        s_lo = shifted(0, col)
        for d in range(1, w_lo):
            s_lo = s_lo + shifted(d, col)
        s_hi = s_lo
        for d in range(w_lo, w_hi):
            s_hi = s_hi + shifted(d, col)
        win_sum = jnp.where(low, s_lo, s_hi)
        count = jnp.minimum(t_seq, jnp.where(low, float(w_lo), float(w_hi)))
        mixed.append(win_sum / count - p[:, col * LANES:(col + 1) * LANES])
    mixed = jnp.concatenate(mixed, axis=1)
    ext_ref[0:POOL_HALO, :] = ext_ref[tt:tt + POOL_HALO, :]
    pool = _dot(mixed.astype(MXU_DTYPE), wpool_ref[...]) * pscale_ref[...]
    pool_ref[0] = pool.astype(pool_ref.dtype)

    ang_t = invf_ref[...] * pos_ref[0].astype(F32)
    cos_t = jnp.cos(ang_t)
    sin_t = jnp.sin(ang_t)
    cos_t = jnp.concatenate([cos_t, cos_t], axis=0)
    sin_t = jnp.concatenate([sin_t, sin_t], axis=0)

    o1 = D_POOL
    o2 = o1 + Q_LORA
    o3 = o2 + KV_LORA
    cqn = _rms_norm(z[:, o1:o2], qg_ref[...]).astype(MXU_DTYPE)
    qq_t = _dot_nt(wuqt_ref[...], cqn)
    n_nope = N_HEADS * QK_NOPE
    n_rope = N_HEADS * QK_ROPE
    scale = QK_DIM ** -0.5 * LOG2_E
    cw = qt_ref.shape[4]
    for hd in range(N_HEADS):
        r0 = n_nope + hd * QK_ROPE
        rope_t = qq_t[r0:r0 + QK_ROPE] * cos_t + qq_t[r0 + n_rope:r0 + n_rope + QK_ROPE] * sin_t
        qh_t = (jnp.concatenate([qq_t[hd * QK_NOPE:(hd + 1) * QK_NOPE], rope_t], axis=0)
                * scale).astype(qt_ref.dtype)
        for cc in range(tt // cw):
            qt_ref[0, hd, cc] = qh_t[:, cc * cw:(cc + 1) * cw]

    ckvn = _rms_norm(z[:, o2:o3], kvg_ref[...]).astype(MXU_DTYPE)
    k_nope = _dot(ckvn, wuk_ref[...])
    v_t = _dot_nt(wuvt_ref[...], ckvn)
    cs = jnp.concatenate([cos_t, sin_t], axis=0).T
    kk = z[:, o3:] * cs
    k_rope = (kk + pltpu.roll(kk, QK_ROPE, 1))[:, :QK_ROPE]
    for hd in range(N_HEADS):
        kh = jnp.concatenate([k_nope[:, hd * QK_NOPE:(hd + 1) * QK_NOPE], k_rope], axis=1)
        k_ref[0, hd] = kh.astype(k_ref.dtype)
        vt_ref[0, hd, 0] = v_t[hd * V_HEAD:(hd + 1) * V_HEAD].astype(vt_ref.dtype)


def _rotate_half_cols(w):
    half = w.shape[-1] // 2
    return jnp.concatenate([-w[..., half:], w[..., :half]], axis=-1)


def _inproj(x, mod, positions, w_in, pool_w, pool_scale, q_norm_g, w_uq, kv_norm_g, w_ukv, tt, cw):
    bsz, seq, d = x.shape
    o3 = D_POOL + Q_LORA + KV_LORA
    w_in_x = jnp.concatenate([w_in, _rotate_half_cols(w_in[:, o3:])], axis=1).astype(MXU_DTYPE)
    n_groups = len(POOL_WINDOWS)
    w_pool_bd = jnp.zeros((D_POOL, D_POOL), F32)
    for g in range(n_groups):
        w_pool_bd = w_pool_bd.at[g * POOL_GROUP:(g + 1) * POOL_GROUP,
                                 g * POOL_GROUP:(g + 1) * POOL_GROUP].set(pool_w[g])
    w_pool_bd = w_pool_bd.astype(MXU_DTYPE)
    w_uq3 = w_uq.reshape(Q_LORA, N_HEADS, QK_DIM)
    w_q_rope = w_uq3[:, :, QK_NOPE:]
    w_uq_t = jnp.concatenate([w_uq3[:, :, :QK_NOPE].reshape(Q_LORA, N_HEADS * QK_NOPE),
                              w_q_rope.reshape(Q_LORA, N_HEADS * QK_ROPE),
                              _rotate_half_cols(w_q_rope).reshape(Q_LORA, N_HEADS * QK_ROPE)],
                             axis=1).T.astype(MXU_DTYPE)
    w_ukv3 = w_ukv.reshape(KV_LORA, N_HEADS, QK_NOPE + V_HEAD)
    w_uk = w_ukv3[:, :, :QK_NOPE].reshape(KV_LORA, N_HEADS * QK_NOPE).astype(MXU_DTYPE)
    w_uv_t = w_ukv3[:, :, QK_NOPE:].reshape(KV_LORA, N_HEADS * V_HEAD).T.astype(MXU_DTYPE)
    inv_freq = ROPE_THETA ** (-jnp.arange(0, QK_ROPE, 2, dtype=F32) / QK_ROPE)
    invf = inv_freq.reshape(QK_ROPE // 2, 1)
    full = lambda shape: pl.BlockSpec(shape, lambda b, j: (0,) * len(shape))
    return pl.pallas_call(
        _inproj_kernel,
        grid=(bsz, seq // tt),
        in_specs=[pl.BlockSpec((1, tt, d), lambda b, j: (b, j, 0)),
                  pl.BlockSpec((1, 6, d), lambda b, j: (b, 0, 0)),
                  pl.BlockSpec((1, 1, tt), lambda b, j: (b, 0, j)),
                  full(invf.shape),
                  full(w_in_x.shape),
                  full((D_POOL, D_POOL)),
                  full((1, D_POOL)),
                  full((1, Q_LORA)),
                  full(w_uq_t.shape),
                  full((1, KV_LORA)),
                  full(w_uk.shape),
                  full(w_uv_t.shape)],
        out_specs=[pl.BlockSpec((1, tt, D_POOL), lambda b, j: (b, j, 0)),
                   pl.BlockSpec((1, N_HEADS, tt // cw, QK_DIM, cw), lambda b, j: (b, 0, j, 0, 0)),
                   pl.BlockSpec((1, N_HEADS, tt, QK_DIM), lambda b, j: (b, 0, j, 0)),
                   pl.BlockSpec((1, N_HEADS, 1, V_HEAD, tt), lambda b, j: (b, 0, j, 0, 0))],
        out_shape=[jax.ShapeDtypeStruct((bsz, seq, D_POOL), MXU_DTYPE),
                   jax.ShapeDtypeStruct((bsz, N_HEADS, seq // cw, QK_DIM, cw), MXU_DTYPE),
                   jax.ShapeDtypeStruct((bsz, N_HEADS, seq, QK_DIM), MXU_DTYPE),
                   jax.ShapeDtypeStruct((bsz, N_HEADS, seq // tt, V_HEAD, tt), MXU_DTYPE)],
        scratch_shapes=[pltpu.VMEM((tt + POOL_HALO, D_POOL), F32)],
        compiler_params=pltpu.CompilerParams(dimension_semantics=("arbitrary", "arbitrary"),
                                             vmem_limit_bytes=VMEM_LIMIT_BYTES),
        name="in_proj",
    )(x, mod.reshape(bsz, 6, d), positions.reshape(bsz, 1, seq), invf, w_in_x, w_pool_bd,
      pool_scale.reshape(1, D_POOL), q_norm_g.reshape(1, Q_LORA), w_uq_t,
      kv_norm_g.reshape(1, KV_LORA), w_uk, w_uv_t)


def _attn_kernel(qt_ref, k_ref, vt_ref, o_ref, sa_ref, sb_ref, m_ref, l_ref, acc_ref, *, tq, tk, cw):
    seq = k_ref.shape[2]
    n_chunks = tq // cw
    assert tq == 2 * tk

    def q_block(qi, carry):
        q0 = pl.multiple_of(qi * tq, tq)
        m_ref[...] = jnp.full(m_ref.shape, -jnp.inf, F32)
        l_ref[...] = jnp.zeros(l_ref.shape, F32)
        acc_ref[...] = jnp.zeros(acc_ref.shape, F32)

        def active_chunks(rel):
            return [c for c in range(n_chunks) if rel is None or (c + 1) * cw > rel * tk]

        def scores_into(buf_ref, kj, rel):
            k0 = pl.multiple_of(kj * tk, tk)
            k = k_ref[0, 0, pl.ds(k0, tk), :]
            for c in active_chunks(rel):
                buf_ref[c] = _dot(k, qt_ref[0, 0, qi * n_chunks + c])

        def consume(buf_ref, kj, rel):
            vt = vt_ref[0, 0, kj]
            probs, alphas = {}, {}
            for c in active_chunks(rel):
                cols = slice(c * cw, (c + 1) * cw)
                s = buf_ref[c]
                if rel is not None and c * cw < (rel + 1) * tk:
                    kv_idx = rel * tk + lax.broadcasted_iota(jnp.int32, (tk, cw), 0)
                    q_idx = c * cw + lax.broadcasted_iota(jnp.int32, (tk, cw), 1)
                    s = jnp.where(q_idx >= kv_idx, s, -jnp.inf)
                m_prev = m_ref[:, cols]
                m_new = jnp.maximum(m_prev, jnp.max(s, axis=0, keepdims=True))
                alphas[c] = jnp.exp2(m_prev - m_new)
                p = jnp.exp2(s - m_new)
                l_ref[:, cols] = alphas[c] * l_ref[:, cols] + jnp.sum(p, axis=0, keepdims=True)
                m_ref[:, cols] = m_new
                probs[c] = p.astype(MXU_DTYPE)
            for c in active_chunks(rel):
                cols = slice(c * cw, (c + 1) * cw)
                acc_ref[:, cols] = alphas[c] * acc_ref[:, cols] + _dot(vt, probs[c])

        def pair(jj, c):
            scores_into(sb_ref, 2 * jj + 1, None)
            consume(sa_ref, 2 * jj, None)
            scores_into(sa_ref, 2 * jj + 2, None)
            consume(sb_ref, 2 * jj + 1, None)
            return c

        def two_pairs(jq, c):
            pair(2 * jq, c)
            pair(2 * jq + 1, c)
            return c

        scores_into(sa_ref, 0, None)
        lax.fori_loop(0, qi // 2, two_pairs, 0)

        @pl.when(qi % 2 == 1)
        def _():
            pair(qi - 1, 0)

        scores_into(sb_ref, 2 * qi + 1, 1)
        consume(sa_ref, 2 * qi, 0)
        consume(sb_ref, 2 * qi + 1, 1)
        out_t = acc_ref[...] / l_ref[...]
        o_ref[0, pl.ds(q0, tq), :] = out_t.T.astype(o_ref.dtype)
        return carry

    lax.fori_loop(0, seq // tq, q_block, 0)


def _attention(qt, k, vt, tq, cw):
    bsz, nh, seq, _ = k.shape
    tk = vt.shape[4]
    return pl.pallas_call(
        functools.partial(_attn_kernel, tq=tq, tk=tk, cw=cw),
        grid=(bsz, nh),
        in_specs=[pl.BlockSpec((1, 1, seq // cw, QK_DIM, cw), lambda b, h: (b, h, 0, 0, 0)),
                  pl.BlockSpec((1, 1, seq, QK_DIM), lambda b, h: (b, h, 0, 0)),
                  pl.BlockSpec((1, 1, seq // tk, V_HEAD, tk), lambda b, h: (b, h, 0, 0, 0))],
        out_specs=pl.BlockSpec((1, seq, V_HEAD), lambda b, h: (b, 0, h)),
        out_shape=jax.ShapeDtypeStruct((bsz, seq, nh * V_HEAD), MXU_DTYPE),
        scratch_shapes=[pltpu.VMEM((tq // cw, tk, cw), F32),
                        pltpu.VMEM((tq // cw, tk, cw), F32),
                        pltpu.VMEM((1, tq), F32),
                        pltpu.VMEM((1, tq), F32),
                        pltpu.VMEM((V_HEAD, tq), F32)],
        compiler_params=pltpu.CompilerParams(dimension_semantics=("arbitrary", "arbitrary"),
                                             vmem_limit_bytes=VMEM_LIMIT_BYTES),
        name="mla_attention",
    )(qt, k, vt)


def _outproj_kernel(x_ref, pool_ref, att_ref, mod_ref, wout_ref, g_ref, b_ref, x1_ref, h2t_ref):
    mix = _dot(pool_ref[0], wout_ref[0:D_POOL, :]) + _dot(att_ref[0], wout_ref[D_POOL:, :])
    g1 = mod_ref[0, 2:3, :]
    sh2 = mod_ref[0, 3:4, :]
    sc2 = mod_ref[0, 4:5, :]
    x1 = _layer_norm(ALPHA * x_ref[0] + g1 * mix, g_ref[...], b_ref[...])
    x1_ref[0] = x1
    h2t_ref[0] = (x1 * (1.0 + sc2) + sh2).T.astype(h2t_ref.dtype)


def _outproj(x, pool, att, mod, w_out, ln_g, ln_b, tt):
    bsz, seq, d = x.shape
    tile = lambda n: pl.BlockSpec((1, tt, n), lambda b, j: (b, j, 0))
    full = lambda shape: pl.BlockSpec(shape, lambda b, j: (0,) * len(shape))
    return pl.pallas_call(
        _outproj_kernel,
        grid=(bsz, seq // tt),
        in_specs=[tile(d), tile(D_POOL), tile(D_ATT),
                  pl.BlockSpec((1, 6, d), lambda b, j: (b, 0, 0)),
                  full(w_out.shape), full((1, d)), full((1, d))],
        out_specs=[tile(d), pl.BlockSpec((1, d, tt), lambda b, j: (b, 0, j))],
        out_shape=[jax.ShapeDtypeStruct((bsz, seq, d), F32),
                   jax.ShapeDtypeStruct((bsz, d, seq), MXU_DTYPE)],
        compiler_params=pltpu.CompilerParams(dimension_semantics=("arbitrary", "arbitrary"),
                                             vmem_limit_bytes=VMEM_LIMIT_BYTES),
        name="out_proj_ln1",
    )(x, pool, att, mod.reshape(bsz, 6, d), w_out.astype(MXU_DTYPE), ln_g.reshape(1, d),
      ln_b.reshape(1, d))


def _extract_topk(s, k, fence=None, with_rank=True):
    n = s.shape[0]
    iota = lax.broadcasted_iota(jnp.int32, s.shape, 0).astype(F32)
    rank = jnp.full(s.shape, NOT_SELECTED, F32)
    vals = []
    sub = lax.broadcasted_iota(jnp.int32, (SUBLANES, s.shape[1]), 0).astype(F32)
    for r in range(k):
        zeros = fence() if fence is not None else None
        if zeros is not None:
            s = s + jnp.concatenate([zeros] * (n // SUBLANES), axis=0)
        nodes = [(s[g:g + SUBLANES], float(g)) for g in range(0, n, SUBLANES)]
        while len(nodes) > 1:
            merged = []
            for a in range(0, len(nodes) - 1, 2):
                (va, ga), (vb, gb) = nodes[a], nodes[a + 1]
                take_b = vb > va
                merged.append((jnp.where(take_b, vb, va), jnp.where(take_b, gb, ga)))
            if len(nodes) % 2:
                merged.append(nodes[-1])
            nodes = merged
        v8, g8 = nodes[0]
        m = jnp.max(v8, axis=0, keepdims=True)
        first = jnp.min(jnp.where(v8 == m, g8 + sub, float(n)), axis=0, keepdims=True)
        sel = iota == first
        if with_rank:
            rank = jnp.where(sel, float(r + 1), rank)
        s = jnp.where(sel, -jnp.inf, s)
        vals.append(m)
        yield m
    return (rank if with_rank else s == -jnp.inf), jnp.concatenate(vals, axis=0)


def _zeros_after(x):
    bits = lax.bitcast_convert_type(x, jnp.uint32)
    bits = lax.shift_right_logical(lax.shift_right_logical(bits, jnp.uint32(16)), jnp.uint32(16))
    return lax.bitcast_convert_type(bits, F32)


def _peer_kernel(h2t_ref, h2t_next_ref, x1_ref, mod_ref, wq_ref, keys_ref, u_ref, vt_ref, g_ref, b_ref,
                 o_ref, lim1_ref, e1_ref, r2_ref, e2_ref, gate_ref, sc_ref, a0_ref, a1_ref, w0_ref, w1_ref,
                 acc_ref):
    a_refs = (a0_ref, a1_ref)
    w_refs = (w0_ref, w1_ref)
    t = pl.program_id(0)
    c = pl.program_id(1)
    tt = h2t_ref.shape[2]
    k = PEER_TOPK
    slabs = PEER_NKEYS // SLAB_ROWS
    n_tb = tt // LANES
    slot = t % 2

    def head_scores(hd, h2t):
        r0 = pl.multiple_of(hd * 2 * PEER_HALF, 2 * PEER_HALF)
        q_t = _dot(wq_ref[pl.ds(r0, 2 * PEER_HALF), :], h2t).astype(MXU_DTYPE)
        return jnp.stack([_dot(keys_ref[2 * hd], q_t[:PEER_HALF]),
                          _dot(keys_ref[2 * hd + 1], q_t[PEER_HALF:])])

    def head_tables(scores, hd, dst, fence=None):
        s1, s2 = scores[0], scores[1]
        rank1, a = yield from _extract_topk(s1, k, fence)
        rank2, b = yield from _extract_topk(s2, k, fence)
        blocks = [a[0:1] + b]
        blocks += [a[r:r + 1] + b[0:8] for r in range(1, 4)]
        blocks += [a[r:r + 1] + b[0:4] for r in range(4, 8)]
        blocks += [a[8:16] + b[0:1]]
        cand = jnp.concatenate(blocks, axis=0)
        sel, _ = yield from _extract_topk(cand, k, fence, with_rank=False)
        sel_f = jnp.where(sel, 1.0, 0.0)
        row_len = [jnp.sum(sel_f[0:16], axis=0, keepdims=True)]
        row_len += [jnp.sum(sel_f[16 + 8 * r:24 + 8 * r], axis=0, keepdims=True) for r in range(3)]
        row_len += [jnp.sum(sel_f[40 + 4 * r:44 + 4 * r], axis=0, keepdims=True) for r in range(4)]
        row_len += [sel_f[56 + r:57 + r] for r in range(8)]
        top = a[0:1] + b[0:1]
        z = jnp.sum(jnp.where(sel, jnp.exp(cand - top), 0.0), axis=0, keepdims=True)
        yield
        n_tail = jnp.sum(sel_f[56:64], axis=0, keepdims=True)
        lim1 = jnp.where(rank1 <= 8.0 + n_tail, 1.0, 0.0)
        for r in range(8):
            lim1 = jnp.where(rank1 == float(r + 1), row_len[r], lim1)
            if r % 4 == 3:
                yield
        e1 = jnp.exp(s1 - a[0:1]) / z
        e2 = jnp.exp(s2 - b[0:1])
        for tb in range(n_tb):
            ls = slice(tb * LANES, (tb + 1) * LANES)
            lim1_ref[dst, hd, tb] = lim1[:, ls].astype(ROW_TABLE_DTYPE).reshape(slabs, SLAB_ROWS, LANES)
            e1_ref[dst, hd, tb] = e1[:, ls].astype(ROW_TABLE_DTYPE).reshape(slabs, SLAB_ROWS, LANES)
            r2_ref[dst, hd, tb] = rank2[:, ls].reshape(slabs, SLAB_ROWS, LANES)
            e2_ref[dst, hd, tb] = e2[:, ls].reshape(slabs, SLAB_ROWS, LANES)

    @pl.when(c == 0)
    def _():
        acc_ref[...] = jnp.zeros(acc_ref.shape, F32)
        sc_ref[...] = head_scores(0, h2t_next_ref[0])

    @pl.when((c == 0) & (t == 0))
    def _():
        def head(hd, carry):
            for _ in head_tables(head_scores(hd, h2t_ref[0]), hd, 0):
                pass
            return carry

        lax.fori_loop(0, PEER_HEADS, head, 0)

    rows = u_ref.shape[0] // PEER_NKEYS
    group = 4

    def gate_block(tb, carry):
        for row0 in range(0, rows, group):
            accs = [[None] * slabs for _ in range(group)]
            for hd in range(PEER_HEADS):
                lim_t = lim1_ref[slot, hd, tb, c].astype(F32)
                e1_t = e1_ref[slot, hd, tb, c].astype(F32)
                lims = [jnp.broadcast_to(lim_t[row0 + ii:row0 + ii + 1], (SLAB_ROWS, LANES))
                        for ii in range(group)]
                e1rs = [jnp.broadcast_to(e1_t[row0 + ii:row0 + ii + 1], (SLAB_ROWS, LANES))
                        for ii in range(group)]
                for sl in range(slabs):
                    r2 = r2_ref[slot, hd, tb, sl]
                    e2 = e2_ref[slot, hd, tb, sl]
                    for ii in range(group):
                        term = jnp.where(r2 <= lims[ii], e2 * e1rs[ii], 0.0)
                        accs[ii][sl] = term if accs[ii][sl] is None else accs[ii][sl] + term
            for ii in range(group):
                for sl in range(slabs):
                    gate_ref[tb, (row0 + ii) * slabs + sl] = accs[ii][sl]
        return carry

    lax.fori_loop(0, n_tb, gate_block, 0)

    h2t = h2t_ref[0]
    chain_rows = PEER_ROWS_PER_CHAIN
    chain_e = chain_rows * PEER_NKEYS
    n_chains = rows // chain_rows
    fences = []
    retrieval = head_tables(sc_ref[...], c, 1 - slot, lambda: fences.pop() if fences else None)
    n_pieces = 3 * k + 3
    first_slice = k
    per_slice = -(-(n_pieces - first_slice) // n_chains)

    def retrieval_slice(n):
        piece = None
        for _ in range(n):
            piece = next(retrieval, None)
        return piece

    def expert_pre(ch, after=None):
        rhs = h2t
        if after is not None:
            zeros = jnp.broadcast_to(_zeros_after(after), (SLAB_ROWS, tt)).astype(h2t.dtype)
            rhs = jnp.concatenate([h2t[:SLAB_ROWS] + zeros, h2t[SLAB_ROWS:]], axis=0)
        a_refs[ch % 2][...] = _dot(u_ref[ch * chain_e:(ch + 1) * chain_e, :], rhs)

    def gated(ch):
        last = []
        for tb in range(n_tb):
            ls = slice(tb * LANES, (tb + 1) * LANES)
            for sb in range(chain_rows * slabs):
                r0 = sb * SLAB_ROWS
                a_blk = a_refs[ch % 2][r0:r0 + SLAB_ROWS, ls]
                gelu = 0.5 * a_blk * (1.0 + lax.erf(a_blk * (2.0 ** -0.5)))
                w_blk = gate_ref[tb, ch * chain_rows * slabs + sb] * gelu
                w_refs[ch % 2][r0:r0 + SLAB_ROWS, ls] = w_blk.astype(MXU_DTYPE)
            last.append(w_blk[:SUBLANES])
        fences.append(_zeros_after(jnp.concatenate(last, axis=1)))

    def expert_post(ch):
        acc_ref[...] += _dot(vt_ref[:, ch * chain_e:(ch + 1) * chain_e], w_refs[ch % 2][...])

    expert_pre(0)
    first_done = retrieval_slice(first_slice)
    for ch in range(n_chains):
        if ch > 0:
            expert_post(ch - 1)
        if ch + 1 < n_chains:
            expert_pre(ch + 1, first_done if ch == 0 else None)
        gated(ch)
        retrieval_slice(per_slice)
    expert_post(n_chains - 1)
    for _ in retrieval:
        pass
    sc_ref[...] = head_scores(jnp.minimum(c + 1, PEER_HEADS - 1), h2t_next_ref[0])

    @pl.when(c == pl.num_programs(1) - 1)
    def _():
        ffn = acc_ref[...].T
        g2 = mod_ref[0, 5:6, :]
        o_ref[...] = _layer_norm(ALPHA * x1_ref[...] + g2 * ffn, g_ref[...], b_ref[...])


def _peer(h2t, x1, mod, w_peer_q, peer_keys, peer_u, peer_v, ln_g, ln_b, seq, tt):
    n_tok, d = x1.shape
    bsz = n_tok // seq
    rows = PEER_ROWS_PER_CHUNK
    ec = rows * PEER_NKEYS
    n_exp = peer_u.shape[0]
    n_q = w_peer_q.shape[1]
    tiles_per_seq = seq // tt
    n_tiles = n_tok // tt
    assert n_exp // ec == PEER_HEADS and rows == SLAB_ROWS and tt % LANES == 0
    n_tb = tt // LANES
    hk_slab = (2, PEER_HEADS, n_tb, PEER_NKEYS // SLAB_ROWS, SLAB_ROWS, LANES)
    gate_shape = (n_tb, rows * PEER_NKEYS // SLAB_ROWS, SLAB_ROWS, LANES)
    full = lambda shape: pl.BlockSpec(shape, lambda t, c: (0,) * len(shape),
                                      pipeline_mode=pl.Buffered(1))

    def h2t_spec(offset):
        def index(t, c):
            tn = jnp.minimum(t + offset, n_tiles - 1)
            return (tn // tiles_per_seq, 0, tn % tiles_per_seq)
        return pl.BlockSpec((1, d, tt), index)

    return pl.pallas_call(
        _peer_kernel,
        grid=(n_tiles, n_exp // ec),
        in_specs=[h2t_spec(0), h2t_spec(1),
                  pl.BlockSpec((tt, d), lambda t, c: (t, 0)),
                  pl.BlockSpec((1, 6, d), lambda t, c: (t // tiles_per_seq, 0, 0)),
                  full((n_q, d)),
                  full((2 * PEER_HEADS, PEER_NKEYS, PEER_HALF)),
                  pl.BlockSpec((ec, d), lambda t, c: (c, 0)),
                  pl.BlockSpec((d, ec), lambda t, c: (0, c)),
                  full((1, d)), full((1, d))],
        out_specs=pl.BlockSpec((tt, d), lambda t, c: (t, 0)),
        out_shape=jax.ShapeDtypeStruct((n_tok, d), F32),
        scratch_shapes=[pltpu.VMEM(hk_slab, ROW_TABLE_DTYPE), pltpu.VMEM(hk_slab, ROW_TABLE_DTYPE),
                        pltpu.VMEM(hk_slab, F32), pltpu.VMEM(hk_slab, F32),
                        pltpu.VMEM(gate_shape, F32),
                        pltpu.VMEM((2, PEER_NKEYS, tt), F32),
                        pltpu.VMEM((PEER_ROWS_PER_CHAIN * PEER_NKEYS, tt), F32),
                        pltpu.VMEM((PEER_ROWS_PER_CHAIN * PEER_NKEYS, tt), F32),
                        pltpu.VMEM((PEER_ROWS_PER_CHAIN * PEER_NKEYS, tt), MXU_DTYPE),
                        pltpu.VMEM((PEER_ROWS_PER_CHAIN * PEER_NKEYS, tt), MXU_DTYPE),
                        pltpu.VMEM((d, tt), F32)],
        compiler_params=pltpu.CompilerParams(dimension_semantics=("arbitrary", "arbitrary"),
                                             vmem_limit_bytes=VMEM_LIMIT_BYTES),
        name="peer_ln2",
    )(h2t, h2t, x1, mod.reshape(bsz, 6, d), w_peer_q.T.astype(MXU_DTYPE),
      peer_keys.reshape(2 * PEER_HEADS, PEER_NKEYS, PEER_HALF).astype(MXU_DTYPE),
      peer_u.astype(MXU_DTYPE), peer_v.T.astype(MXU_DTYPE), ln_g.reshape(1, d), ln_b.reshape(1, d))


def kernel(x, c, positions, w_ada, b_ada, w_in, pool_w, pool_scale, q_norm_g, w_uq, kv_norm_g,
           w_ukv, w_out, ln1_g, ln1_b, w_peer_q, peer_keys, peer_u, peer_v, ln2_g, ln2_b):
    bsz, seq, d = x.shape
    assert w_ada.shape[0] == DEPTH
    tt = min(TOKEN_TILE, seq)
    tq = min(ATTN_Q_BLOCK, seq)
    cw = min(ATTN_CHUNK, tt)
    peer_tt = min(PEER_TOKEN_TILE, seq)
    for l in range(DEPTH):
        mod = _adaln(c, w_ada[l], b_ada[l])
        pool, qt, k, vt = _inproj(x, mod, positions, w_in[l], pool_w[l], pool_scale[l],
                                  q_norm_g[l], w_uq[l], kv_norm_g[l], w_ukv[l], tt, cw)
        att = _attention(qt, k, vt, tq, cw)
        x1, h2 = _outproj(x, pool, att, mod, w_out[l], ln1_g[l], ln1_b[l], tt)
        x = _peer(h2, x1.reshape(bsz * seq, d), mod, w_peer_q[l],
                  peer_keys[l], peer_u[l], peer_v[l], ln2_g[l], ln2_b[l], seq,
                  peer_tt).reshape(bsz, seq, d)
    return x
```
